```python
import jax, jax.numpy as jnp
from jax import lax
import numpy as np

D_MODEL = 1024
BATCH = 2
SEQ = 8192
DEPTH = 2

EXPAND = 2
D_INNER = EXPAND * D_MODEL
POOL_WINDOWS = (2, 4, 8, 16)
N_POOL_GROUPS = len(POOL_WINDOWS)
POOL_GROUP_DIM = D_INNER // N_POOL_GROUPS
HEAD_DIM = 128
N_HEADS = D_INNER // HEAD_DIM
MOBA_BLOCK = 256
MOBA_TOPK = 3
Q_CHUNK = 16
ROPE_THETA = 10000.0
EPS = 1e-6
N_MIXERS = 2

kernel_name = "hybrid_pool_moba_gated"


def rms_norm(x, g):
    xf = x.astype(jnp.float32)
    y = xf * lax.rsqrt(jnp.mean(xf * xf, axis=-1, keepdims=True) + EPS)
    return (y * g.astype(jnp.float32)).astype(x.dtype)


def rope(x):
    S = x.shape[2]
    inv = ROPE_THETA ** (-jnp.arange(0, HEAD_DIM, 2, dtype=jnp.float32) / HEAD_DIM)
    ang = jnp.arange(S, dtype=jnp.float32)[:, None] * inv[None, :]
    cos, sin = jnp.cos(ang), jnp.sin(ang)
    xf = x.astype(jnp.float32)
    x1, x2 = xf[..., : HEAD_DIM // 2], xf[..., HEAD_DIM // 2:]
    out = jnp.concatenate([x1 * cos - x2 * sin, x2 * cos + x1 * sin], axis=-1)
    return out.astype(x.dtype)


def pool_mixer(u, w_pool, pool_scale):
    B, S, _ = u.shape
    uf = u.astype(jnp.float32).reshape(B, S, N_POOL_GROUPS, POOL_GROUP_DIM)
    cs = jnp.cumsum(uf, axis=1)
    pos = jnp.arange(1, S + 1, dtype=jnp.float32)
    outs = []
    for g, w in enumerate(POOL_WINDOWS):
        cg = cs[:, :, g]
        lag = jnp.pad(cg, ((0, 0), (w, 0), (0, 0)))[:, :S]
        cnt = jnp.minimum(pos, float(w))[None, :, None]
        outs.append((cg - lag) / cnt - uf[:, :, g])
    pooled = jnp.stack(outs, axis=2).astype(u.dtype)
    mixed = jnp.einsum('bsgc,gcd->bsgd', pooled, w_pool).reshape(B, S, D_INNER)
    return mixed * pool_scale


def pool_layer(x, norm_g, w_in, w_pool, pool_scale, w_out):
    h = rms_norm(x, norm_g)
    u, z = jnp.split(h @ w_in, 2, axis=-1)
    y = pool_mixer(u, w_pool, pool_scale) * jax.nn.silu(z)
    return x + y @ w_out


def moba_attention(q, k, v):
    B, H, S, dh = q.shape
    nb = -(-S // MOBA_BLOCK)
    pad = nb * MOBA_BLOCK - S
    kp = jnp.pad(k, ((0, 0), (0, 0), (0, pad), (0, 0)))
    vp = jnp.pad(v, ((0, 0), (0, 0), (0, pad), (0, 0)))
    kb = kp.reshape(B, H, nb, MOBA_BLOCK, dh)
    vb = vp.reshape(B, H, nb, MOBA_BLOCK, dh)
    k_mean = jnp.mean(kb.astype(jnp.float32), axis=3)

    q_blk = jnp.arange(S) // MOBA_BLOCK
    gate = jnp.einsum('bhsd,bhnd->bhsn', q.astype(jnp.float32), k_mean)
    past = jnp.arange(nb)[None, :] < q_blk[:, None]
    gate = jnp.where(past, gate, -jnp.inf)
    n_sel = min(MOBA_TOPK, nb)
    _, sel = lax.top_k(gate, n_sel)
    sel_valid = sel < q_blk[:, None]

    nc = S // Q_CHUNK

    def to_chunks(t):
        return jnp.moveaxis(t.reshape(B, H, nc, Q_CHUNK, *t.shape[3:]), 2, 0)

    scale = HEAD_DIM ** -0.5
    gather = jax.vmap(jax.vmap(lambda blocks, idx: blocks[idx]))

    def chunk_fn(args):
        c, qc, selc, validc = args
        start = c * Q_CHUNK
        own = start // MOBA_BLOCK
        k_own = lax.dynamic_slice_in_dim(kp, own * MOBA_BLOCK, MOBA_BLOCK, axis=2)
        v_own = lax.dynamic_slice_in_dim(vp, own * MOBA_BLOCK, MOBA_BLOCK, axis=2)
        q_pos = start + jnp.arange(Q_CHUNK)
        k_pos = own * MOBA_BLOCK + jnp.arange(MOBA_BLOCK)
        s_own = jnp.einsum('bhqd,bhkd->bhqk', qc, k_own).astype(jnp.float32) * scale
        s_own = jnp.where(k_pos[None, :] <= q_pos[:, None], s_own, -jnp.inf)
        k_sel = gather(kb, selc)
        v_sel = gather(vb, selc)
        s_sel = jnp.einsum('bhqd,bhqnkd->bhqnk', qc, k_sel).astype(jnp.float32) * scale
        s_sel = jnp.where(validc[..., None], s_sel, -jnp.inf)
        s = jnp.concatenate([s_own, s_sel.reshape(B, H, Q_CHUNK, n_sel * MOBA_BLOCK)], axis=-1)
        p = jax.nn.softmax(s, axis=-1).astype(v.dtype)
        p_own = p[..., :MOBA_BLOCK]
        p_sel = p[..., MOBA_BLOCK:].reshape(B, H, Q_CHUNK, n_sel, MOBA_BLOCK)
        return (jnp.einsum('bhqk,bhkd->bhqd', p_own, v_own)
                + jnp.einsum('bhqnk,bhqnkd->bhqd', p_sel, v_sel))

    out = lax.map(chunk_fn, (jnp.arange(nc), to_chunks(q), to_chunks(sel), to_chunks(sel_valid)))
    return jnp.moveaxis(out, 0, 2).reshape(B, H, S, dh)


def moba_layer(x, norm_g, w_in, q_norm, k_norm, w_out):
    B, S, _ = x.shape
    h = rms_norm(x, norm_g)
    q, k, v, z = jnp.split(h @ w_in, 4, axis=-1)

    def heads(t):
        return t.reshape(B, S, N_HEADS, HEAD_DIM).transpose(0, 2, 1, 3)

    q = rope(rms_norm(heads(q), q_norm))
    k = rope(rms_norm(heads(k), k_norm))
    v = heads(v)
    o = moba_attention(q, k, v).transpose(0, 2, 1, 3).reshape(B, S, D_INNER)
    return x + (o * jax.nn.silu(z)) @ w_out


def setup_inputs(seed: int = 0) -> dict:
    key = jax.random.key(seed)
    ks = jax.random.split(key, 12)
    f32 = jnp.float32
    nrm = jax.random.normal
    x = nrm(ks[0], (BATCH, SEQ, D_MODEL), f32)
    norm0 = 1.0 + 0.02 * nrm(ks[1], (D_MODEL,), f32)
    w_in0 = nrm(ks[2], (D_MODEL, 2 * D_INNER), f32) * D_MODEL ** -0.5
    w_pool0 = nrm(ks[3], (N_POOL_GROUPS, POOL_GROUP_DIM, POOL_GROUP_DIM), f32) * POOL_GROUP_DIM ** -0.5
    pool_scale0 = 1.0 + 0.02 * nrm(ks[4], (D_INNER,), f32)
    w_out0 = nrm(ks[5], (D_INNER, D_MODEL), f32) * D_INNER ** -0.5
    norm1 = 1.0 + 0.02 * nrm(ks[6], (D_MODEL,), f32)
    w_in1 = nrm(ks[7], (D_MODEL, 4 * D_INNER), f32) * D_MODEL ** -0.5
    q_norm1 = 1.0 + 0.02 * nrm(ks[8], (HEAD_DIM,), f32)
    k_norm1 = 1.0 + 0.02 * nrm(ks[9], (HEAD_DIM,), f32)
    w_out1 = nrm(ks[10], (D_INNER, D_MODEL), f32) * D_INNER ** -0.5
    return {"x": x,
            "norm0": norm0, "w_in0": w_in0, "w_pool0": w_pool0,
            "pool_scale0": pool_scale0, "w_out0": w_out0,
            "norm1": norm1, "w_in1": w_in1, "q_norm1": q_norm1,
            "k_norm1": k_norm1, "w_out1": w_out1}


def reference(x, norm0, w_in0, w_pool0, pool_scale0, w_out0,
              norm1, w_in1, q_norm1, k_norm1, w_out1):
    layers = (
        (pool_layer, (norm0, w_in0, w_pool0, pool_scale0, w_out0)),
        (moba_layer, (norm1, w_in1, q_norm1, k_norm1, w_out1)),
    )
    for i in range(DEPTH):
        layer_fn, params = layers[i]
        x = layer_fn(x, *params)
    return x
```

```python
import functools
import math

import jax
import jax.numpy as jnp
from jax import lax
from jax.experimental import pallas as pl
from jax.experimental.pallas import tpu as pltpu

D_MODEL = 1024
D_INNER = 2 * D_MODEL
POOL_WINDOWS = (2, 4, 8, 16)
POOL_GROUP_DIM = D_INNER // len(POOL_WINDOWS)
HEAD_DIM = 128
N_HEADS = D_INNER // HEAD_DIM
MOBA_BLOCK = 256
MOBA_TOPK = 3
ROPE_THETA = 10000.0
EPS = 1e-6

POOL_HALO = 16
NEG_BIG = -1e30
VMEM_LIMIT = 56 * 1024 * 1024
SEQ_TILE = 512
OUT_TILE = 1024

F32 = jnp.float32
BF16 = jnp.bfloat16


def _const_spec(shape):
    nd = len(shape)
    return pl.BlockSpec(shape, lambda *_: (0,) * nd, pipeline_mode=pl.Buffered(1))


def _rms_scale(x):
    return lax.rsqrt(jnp.mean(x * x, axis=-1, keepdims=True) + EPS)


def _silu(z):
    return z / (1.0 + jnp.exp(-z))


def _pool_layer_kernel(x_ref, g_ref, win_ref, wpool_ref, pscale_ref, wout_ref, o_ref, ubuf_ref, *, tile):
    t = pl.program_id(1)
    x = x_ref[...]
    h = (x * _rms_scale(x) * g_ref[...]).astype(BF16)
    uz = jnp.dot(h, win_ref[...], preferred_element_type=F32)
    u = uz[:, :D_INNER]
    z = uz[:, D_INNER:]

    @pl.when(t == 0)
    def _():
        ubuf_ref[0:POOL_HALO, :] = jnp.zeros((POOL_HALO, D_INNER), F32)

    ubuf_ref[POOL_HALO:POOL_HALO + tile, :] = u

    pos = t * tile + lax.broadcasted_iota(jnp.int32, (tile, 1), 0) + 1
    mixed = []
    for g, w in enumerate(POOL_WINDOWS):
        lo, hi = g * POOL_GROUP_DIM, (g + 1) * POOL_GROUP_DIM
        ext = ubuf_ref[:, lo:hi]
        acc, width = ext, 1
        while width < w:
            acc = acc + pltpu.roll(acc, width, 0)
            width *= 2
        inv_cnt = 1.0 / jnp.minimum(pos, w).astype(F32)
        pooled = acc[POOL_HALO:, :] * inv_cnt - u[:, lo:hi]
        mixed.append(jnp.dot(pooled.astype(BF16), wpool_ref[g], preferred_element_type=F32))
    mixed = jnp.concatenate(mixed, axis=-1)

    ubuf_ref[0:POOL_HALO, :] = u[tile - POOL_HALO:, :]

    y = (mixed * pscale_ref[...] * _silu(z)).astype(BF16)
    o_ref[...] = x + jnp.dot(y, wout_ref[...], preferred_element_type=F32)


def _pool_layer(x, norm_g, w_in, w_pool, pool_scale, w_out):
    B, S, _ = x.shape
    tile = SEQ_TILE
    return pl.pallas_call(
        functools.partial(_pool_layer_kernel, tile=tile),
        grid=(B, S // tile),
        in_specs=[
            pl.BlockSpec((None, tile, D_MODEL), lambda b, t: (b, t, 0)),
            _const_spec((1, D_MODEL)),
            _const_spec((D_MODEL, 2 * D_INNER)),
            _const_spec((len(POOL_WINDOWS), POOL_GROUP_DIM, POOL_GROUP_DIM)),
            _const_spec((1, D_INNER)),
            _const_spec((D_INNER, D_MODEL)),
        ],
        out_specs=pl.BlockSpec((None, tile, D_MODEL), lambda b, t: (b, t, 0)),
        out_shape=jax.ShapeDtypeStruct((B, S, D_MODEL), F32),
        scratch_shapes=[pltpu.VMEM((POOL_HALO + tile, D_INNER), F32)],
        compiler_params=pltpu.CompilerParams(
            dimension_semantics=("arbitrary", "arbitrary"), vmem_limit_bytes=VMEM_LIMIT),
        name="pool_layer",
    )(x, norm_g.reshape(1, D_MODEL), w_in.astype(BF16), w_pool.astype(BF16),
      pool_scale.reshape(1, D_INNER), w_out.astype(BF16))


def _moba_proj_kernel(x_ref, g_ref, wqT_ref, wk_ref, wvT_ref, wz_ref, qmul_ref, kmul_ref,
                      cosT_ref, sinT_ref, cos_ref, sin_ref,
                      qT_ref, k_ref, vT_ref, gate_ref, kmean_ref, *, tile):
    x = x_ref[...]
    h = (x * _rms_scale(x) * g_ref[...]).astype(BF16)
    nt = (((1,), (1,)), ((), ()))

    qT = lax.dot_general(wqT_ref[...], h, nt, preferred_element_type=F32)
    cosT, sinT = cosT_ref[...], sinT_ref[...]
    half = HEAD_DIM // 2
    for hd in range(N_HEADS):
        qh = qT[hd * HEAD_DIM:(hd + 1) * HEAD_DIM, :]
        qh = qh * lax.rsqrt(jnp.mean(qh * qh, axis=0, keepdims=True) + EPS) * qmul_ref[...]
        q1, q2 = qh[:half, :], qh[half:, :]
        qT_ref[hd * HEAD_DIM:hd * HEAD_DIM + half, :] = (q1 * cosT - q2 * sinT).astype(BF16)
        qT_ref[hd * HEAD_DIM + half:(hd + 1) * HEAD_DIM, :] = (q2 * cosT + q1 * sinT).astype(BF16)

    k = jnp.dot(h, wk_ref[...], preferred_element_type=F32)
    cos2, sin2 = cos_ref[...], sin_ref[...]
    for hd in range(N_HEADS):
        kh = k[:, hd * HEAD_DIM:(hd + 1) * HEAD_DIM]
        kh = kh * _rms_scale(kh) * kmul_ref[...]
        kh = kh * cos2 + pltpu.roll(kh, half, 1) * sin2
        k_ref[:, hd * HEAD_DIM:(hd + 1) * HEAD_DIM] = kh.astype(BF16)
        for blk in range(tile // MOBA_BLOCK):
            kmean_ref[blk:blk + 1, hd * HEAD_DIM:(hd + 1) * HEAD_DIM] = jnp.mean(
                kh[blk * MOBA_BLOCK:(blk + 1) * MOBA_BLOCK, :], axis=0, keepdims=True)

    vT_ref[...] = lax.dot_general(wvT_ref[...], h, nt, preferred_element_type=F32).astype(BF16)
    gate_ref[...] = _silu(jnp.dot(h, wz_ref[...], preferred_element_type=F32)).astype(BF16)


def _rope_tables(S):
    inv = ROPE_THETA ** (-jnp.arange(0, HEAD_DIM, 2, dtype=F32) / HEAD_DIM)
    ang = jnp.arange(S, dtype=F32)[:, None] * inv[None, :]
    return jnp.cos(ang), jnp.sin(ang)


def _moba_proj(x, norm_g, w_in, q_norm, k_norm):
    B, S, _ = x.shape
    tile = SEQ_TILE
    nblk = tile // MOBA_BLOCK
    wq, wk, wv, wz = jnp.split(w_in.astype(BF16), 4, axis=1)
    cos, sin = _rope_tables(S)
    q_gain = q_norm * (HEAD_DIM ** -0.5 * math.log2(math.e))
    qmul = jnp.broadcast_to(q_gain[:, None], (HEAD_DIM, tile))
    kmul = k_norm.reshape(1, HEAD_DIM)
    tok = lambda b, t: (b, t, 0)
    feat = lambda b, t: (b, 0, t)
    return pl.pallas_call(
        functools.partial(_moba_proj_kernel, tile=tile),
        grid=(B, S // tile),
        in_specs=[
            pl.BlockSpec((None, tile, D_MODEL), tok),
            _const_spec((1, D_MODEL)),
            _const_spec((D_INNER, D_MODEL)),
            _const_spec((D_MODEL, D_INNER)),
            _const_spec((D_INNER, D_MODEL)),
            _const_spec((D_MODEL, D_INNER)),
            _const_spec((HEAD_DIM, tile)),
            _const_spec((1, HEAD_DIM)),
            pl.BlockSpec((HEAD_DIM // 2, tile), lambda b, t: (0, t)),
            pl.BlockSpec((HEAD_DIM // 2, tile), lambda b, t: (0, t)),
            pl.BlockSpec((tile, HEAD_DIM), lambda b, t: (t, 0)),
            pl.BlockSpec((tile, HEAD_DIM), lambda b, t: (t, 0)),
        ],
        out_specs=[
            pl.BlockSpec((None, D_INNER, tile), feat),
            pl.BlockSpec((None, tile, D_INNER), tok),
            pl.BlockSpec((None, D_INNER, tile), feat),
            pl.BlockSpec((None, tile, D_INNER), tok),
            pl.BlockSpec((None, None, nblk, D_INNER), lambda b, t: (b, t, 0, 0)),
        ],
        out_shape=[
            jax.ShapeDtypeStruct((B, D_INNER, S), BF16),
            jax.ShapeDtypeStruct((B, S, D_INNER), BF16),
            jax.ShapeDtypeStruct((B, D_INNER, S), BF16),
            jax.ShapeDtypeStruct((B, S, D_INNER), BF16),
            jax.ShapeDtypeStruct((B, S // tile, nblk, D_INNER), F32),
        ],
        compiler_params=pltpu.CompilerParams(
            dimension_semantics=("arbitrary", "arbitrary"), vmem_limit_bytes=VMEM_LIMIT),
        name="moba_proj",
    )(x, norm_g.reshape(1, D_MODEL), wq.T, wk, wv.T, wz, qmul, kmul,
      cos.T, sin.T, jnp.concatenate([cos, cos], axis=1), jnp.concatenate([-sin, sin], axis=1))


def _moba_attn_kernel(qT_ref, k_ref, vT_ref, kmean_ref, gate_ref, y_ref, bias_ref, *, n_blocks):
    i = pl.program_id(2)
    qT = qT_ref[...]
    tq = MOBA_BLOCK

    km = kmean_ref[...]
    km_hi = km.astype(BF16)
    km_lo = (km - km_hi.astype(F32)).astype(BF16)
    gate = (jnp.dot(km_hi, qT, preferred_element_type=F32)
            + jnp.dot(km_lo, qT, preferred_element_type=F32))
    blk = lax.broadcasted_iota(jnp.int32, (n_blocks, tq), 0)
    past = blk < i
    cand = jnp.where(past, gate, -jnp.inf)
    chosen = blk == i
    for _ in range(MOBA_TOPK):
        top = jnp.max(cand, axis=0, keepdims=True)
        first = jnp.min(jnp.where(cand == top, blk, n_blocks), axis=0, keepdims=True)
        pick = blk == first
        chosen = jnp.logical_or(chosen, jnp.logical_and(pick, past))
        cand = jnp.where(pick, -jnp.inf, cand)
    bias_ref[...] = jnp.where(chosen, 0.0, NEG_BIG).astype(F32)

    def scores(j):
        start = pl.multiple_of(j * MOBA_BLOCK, MOBA_BLOCK)
        kj = k_ref[pl.ds(start, MOBA_BLOCK), :]
        return jnp.dot(kj, qT, preferred_element_type=F32), start

    s, start = scores(i)
    key_pos = lax.broadcasted_iota(jnp.int32, (MOBA_BLOCK, tq), 0)
    qry_pos = lax.broadcasted_iota(jnp.int32, (MOBA_BLOCK, tq), 1)
    s = jnp.where(key_pos <= qry_pos, s, NEG_BIG)
    m = jnp.max(s, axis=0, keepdims=True)
    p = jnp.exp2(s - m)
    l = jnp.sum(p, axis=0, keepdims=True)
    acc = jnp.dot(vT_ref[:, pl.ds(start, MOBA_BLOCK)], p.astype(BF16),
                  preferred_element_type=F32)

    def body(j, carry):
        m, l, acc = carry
        s, start = scores(j)
        bj = bias_ref[pl.ds(j, 1), :]
        m_new = jnp.maximum(m, jnp.max(s, axis=0, keepdims=True) + bj)
        alpha = jnp.exp2(m - m_new)
        p = jnp.exp2(s - (m_new - bj))
        l = alpha * l + jnp.sum(p, axis=0, keepdims=True)
        acc = alpha * acc + jnp.dot(vT_ref[:, pl.ds(start, MOBA_BLOCK)], p.astype(BF16),
                                    preferred_element_type=F32)
        return m_new, l, acc

    m, l, acc = lax.fori_loop(0, i, body, (m, l, acc))
    oT = acc * (1.0 / l)
    y_ref[...] = (oT.T * gate_ref[...].astype(F32)).astype(BF16)


def _moba_attn(qT, k, vT, kmean, gate):
    B, S, _ = k.shape
    n_blocks = S // MOBA_BLOCK
    return pl.pallas_call(
        functools.partial(_moba_attn_kernel, n_blocks=n_blocks),
        grid=(B, N_HEADS, n_blocks),
        in_specs=[
            pl.BlockSpec((None, HEAD_DIM, MOBA_BLOCK), lambda b, h, i: (b, h, i)),
            pl.BlockSpec((None, S, HEAD_DIM), lambda b, h, i: (b, 0, h)),
            pl.BlockSpec((None, HEAD_DIM, S), lambda b, h, i: (b, h, 0)),
            pl.BlockSpec((None, n_blocks, HEAD_DIM), lambda b, h, i: (b, 0, h)),
            pl.BlockSpec((None, MOBA_BLOCK, HEAD_DIM), lambda b, h, i: (b, i, h)),
        ],
        out_specs=pl.BlockSpec((None, MOBA_BLOCK, HEAD_DIM), lambda b, h, i: (b, i, h)),
        out_shape=jax.ShapeDtypeStruct((B, S, D_INNER), BF16),
        scratch_shapes=[pltpu.VMEM((n_blocks, MOBA_BLOCK), F32)],
        compiler_params=pltpu.CompilerParams(
            dimension_semantics=("arbitrary", "arbitrary", "arbitrary"), vmem_limit_bytes=VMEM_LIMIT),
        name="moba_attn",
    )(qT, k, vT, kmean, gate)


def _out_proj_kernel(x_ref, y_ref, w_ref, o_ref):
    o_ref[...] = x_ref[...] + jnp.dot(y_ref[...], w_ref[...], preferred_element_type=F32)


def _out_proj(x, y, w_out):
    B, S, _ = x.shape
    rows = B * S
    return pl.pallas_call(
        _out_proj_kernel,
        grid=(rows // OUT_TILE,),
        in_specs=[
            pl.BlockSpec((OUT_TILE, D_MODEL), lambda t: (t, 0)),
            pl.BlockSpec((OUT_TILE, D_INNER), lambda t: (t, 0)),
            _const_spec((D_INNER, D_MODEL)),
        ],
        out_specs=pl.BlockSpec((OUT_TILE, D_MODEL), lambda t: (t, 0)),
        out_shape=jax.ShapeDtypeStruct((rows, D_MODEL), F32),
        compiler_params=pltpu.CompilerParams(
            dimension_semantics=("arbitrary",), vmem_limit_bytes=VMEM_LIMIT),
        name="moba_out_proj",
    )(x.reshape(rows, D_MODEL), y.reshape(rows, D_INNER), w_out.astype(BF16)).reshape(B, S, D_MODEL)


def kernel(x, norm0, w_in0, w_pool0, pool_scale0, w_out0, norm1, w_in1, q_norm1, k_norm1, w_out1):
    B, S, _ = x.shape
    x1 = _pool_layer(x, norm0, w_in0, w_pool0, pool_scale0, w_out0)
    qT, k, vT, gate, kmean = _moba_proj(x1, norm1, w_in1, q_norm1, k_norm1)
    y = _moba_attn(qT, k, vT, kmean.reshape(B, S // MOBA_BLOCK, D_INNER), gate)
    return _out_proj(x1, y, w_out1)
```

```python
import functools
import math

import jax
import jax.numpy as jnp
import numpy as np
from jax import lax
from jax.experimental import pallas as pl
from jax.experimental.pallas import tpu as pltpu

D_MODEL = 1024
D_INNER = 2 * D_MODEL
POOL_WINDOWS = (2, 4, 8, 16)
POOL_GROUP_DIM = D_INNER // len(POOL_WINDOWS)
HEAD_DIM = 128
N_HEADS = D_INNER // HEAD_DIM
MOBA_BLOCK = 256
MOBA_TOPK = 3
ROPE_THETA = 10000.0
EPS = 1e-6

POOL_HALO = 16
NEG_BIG = -1e30
VMEM_LIMIT = 56 * 1024 * 1024
SEQ_TILE = 512
OUT_TILE = 1024

F32 = jnp.float32
BF16 = jnp.bfloat16


def _const_spec(shape):
    nd = len(shape)
    return pl.BlockSpec(shape, lambda *_: (0,) * nd, pipeline_mode=pl.Buffered(1))


def _rms_scale(x):
    return lax.rsqrt(jnp.mean(x * x, axis=-1, keepdims=True) + EPS)


def _silu(z):
    return z / (1.0 + jnp.exp(-z))


def _pool_layer_kernel(x_ref, g_ref, win_ref, wpool_ref, pscale_ref, wout_ref, o_ref, ubuf_ref, *, tile):
    t = pl.program_id(1)
    x = x_ref[...]
    h = (x * _rms_scale(x) * g_ref[...]).astype(BF16)
    uz = jnp.dot(h, win_ref[...], preferred_element_type=F32)
    u = uz[:, :D_INNER]
    z = uz[:, D_INNER:]

    @pl.when(t == 0)
    def _():
        ubuf_ref[0:POOL_HALO, :] = jnp.zeros((POOL_HALO, D_INNER), F32)

    ubuf_ref[POOL_HALO:POOL_HALO + tile, :] = u

    pos = t * tile + lax.broadcasted_iota(jnp.int32, (tile, 1), 0) + 1
    mixed = []
    for g, w in enumerate(POOL_WINDOWS):
        lo, hi = g * POOL_GROUP_DIM, (g + 1) * POOL_GROUP_DIM
        ext = ubuf_ref[:, lo:hi]
        acc, width = ext, 1
        while width < w:
            acc = acc + pltpu.roll(acc, width, 0)
            width *= 2
        inv_cnt = 1.0 / jnp.minimum(pos, w).astype(F32)
        pooled = acc[POOL_HALO:, :] * inv_cnt - u[:, lo:hi]
        mixed.append(jnp.dot(pooled.astype(BF16), wpool_ref[g], preferred_element_type=F32))
    mixed = jnp.concatenate(mixed, axis=-1)

    ubuf_ref[0:POOL_HALO, :] = u[tile - POOL_HALO:, :]

    y = (mixed * pscale_ref[...] * _silu(z)).astype(BF16)
    o_ref[...] = x + jnp.dot(y, wout_ref[...], preferred_element_type=F32)


def _pool_layer(x, norm_g, w_in, w_pool, pool_scale, w_out):
    B, S, _ = x.shape
    tile = SEQ_TILE
    return pl.pallas_call(
        functools.partial(_pool_layer_kernel, tile=tile),
        grid=(B, S // tile),
        in_specs=[
            pl.BlockSpec((None, tile, D_MODEL), lambda b, t: (b, t, 0)),
            _const_spec((1, D_MODEL)),
            _const_spec((D_MODEL, 2 * D_INNER)),
            _const_spec((len(POOL_WINDOWS), POOL_GROUP_DIM, POOL_GROUP_DIM)),
            _const_spec((1, D_INNER)),
            _const_spec((D_INNER, D_MODEL)),
        ],
        out_specs=pl.BlockSpec((None, tile, D_MODEL), lambda b, t: (b, t, 0)),
        out_shape=jax.ShapeDtypeStruct((B, S, D_MODEL), F32),
        scratch_shapes=[pltpu.VMEM((POOL_HALO + tile, D_INNER), F32)],
        compiler_params=pltpu.CompilerParams(
            dimension_semantics=("arbitrary", "arbitrary"), vmem_limit_bytes=VMEM_LIMIT),
        name="pool_layer",
    )(x, norm_g.reshape(1, D_MODEL), w_in.astype(BF16), w_pool.astype(BF16),
      pool_scale.reshape(1, D_INNER), w_out.astype(BF16))


def _moba_proj_kernel(x_ref, g_ref, wqT_ref, wk_ref, wvT_ref, wz_ref, qmul_ref, kmul_ref,
                      cosT_ref, sinT_ref, cos_ref, sin_ref,
                      qT_ref, k_ref, vT_ref, gate_ref, kmean_ref, *, tile):
    x = x_ref[...]
    h = (x * _rms_scale(x) * g_ref[...]).astype(BF16)
    nt = (((1,), (1,)), ((), ()))

    qT = lax.dot_general(wqT_ref[...], h, nt, preferred_element_type=F32)
    cosT, sinT = cosT_ref[...], sinT_ref[...]
    half = HEAD_DIM // 2
    for hd in range(N_HEADS):
        qh = qT[hd * HEAD_DIM:(hd + 1) * HEAD_DIM, :]
        qh = qh * lax.rsqrt(jnp.mean(qh * qh, axis=0, keepdims=True) + EPS) * qmul_ref[...]
        q1, q2 = qh[:half, :], qh[half:, :]
        qT_ref[hd * HEAD_DIM:hd * HEAD_DIM + half, :] = (q1 * cosT - q2 * sinT).astype(BF16)
        qT_ref[hd * HEAD_DIM + half:(hd + 1) * HEAD_DIM, :] = (q2 * cosT + q1 * sinT).astype(BF16)

    k = jnp.dot(h, wk_ref[...], preferred_element_type=F32)
    cos2, sin2 = cos_ref[...], sin_ref[...]
    for hd in range(N_HEADS):
        kh = k[:, hd * HEAD_DIM:(hd + 1) * HEAD_DIM]
        kh = kh * _rms_scale(kh) * kmul_ref[...]
        kh = kh * cos2 + pltpu.roll(kh, half, 1) * sin2
        k_ref[:, hd * HEAD_DIM:(hd + 1) * HEAD_DIM] = kh.astype(BF16)
        for blk in range(tile // MOBA_BLOCK):
            kmean_ref[blk:blk + 1, hd * HEAD_DIM:(hd + 1) * HEAD_DIM] = jnp.mean(
                kh[blk * MOBA_BLOCK:(blk + 1) * MOBA_BLOCK, :], axis=0, keepdims=True)

    vT_ref[...] = lax.dot_general(wvT_ref[...], h, nt, preferred_element_type=F32).astype(BF16)
    gate_ref[...] = _silu(jnp.dot(h, wz_ref[...], preferred_element_type=F32)).astype(BF16)


def _rope_tables(S):
    inv = ROPE_THETA ** (-jnp.arange(0, HEAD_DIM, 2, dtype=F32) / HEAD_DIM)
    ang = jnp.arange(S, dtype=F32)[:, None] * inv[None, :]
    return jnp.cos(ang), jnp.sin(ang)


def _moba_proj(x, norm_g, w_in, q_norm, k_norm):
    B, S, _ = x.shape
    tile = SEQ_TILE
    nblk = tile // MOBA_BLOCK
    wq, wk, wv, wz = jnp.split(w_in.astype(BF16), 4, axis=1)
    cos, sin = _rope_tables(S)
    q_gain = q_norm * (HEAD_DIM ** -0.5 * math.log2(math.e))
    qmul = jnp.broadcast_to(q_gain[:, None], (HEAD_DIM, tile))
    kmul = k_norm.reshape(1, HEAD_DIM)
    tok = lambda b, t: (b, t, 0)
    feat = lambda b, t: (b, 0, t)
    return pl.pallas_call(
        functools.partial(_moba_proj_kernel, tile=tile),
        grid=(B, S // tile),
        in_specs=[
            pl.BlockSpec((None, tile, D_MODEL), tok),
            _const_spec((1, D_MODEL)),
            _const_spec((D_INNER, D_MODEL)),
            _const_spec((D_MODEL, D_INNER)),
            _const_spec((D_INNER, D_MODEL)),
            _const_spec((D_MODEL, D_INNER)),
            _const_spec((HEAD_DIM, tile)),
            _const_spec((1, HEAD_DIM)),
            pl.BlockSpec((HEAD_DIM // 2, tile), lambda b, t: (0, t)),
            pl.BlockSpec((HEAD_DIM // 2, tile), lambda b, t: (0, t)),
            pl.BlockSpec((tile, HEAD_DIM), lambda b, t: (t, 0)),
            pl.BlockSpec((tile, HEAD_DIM), lambda b, t: (t, 0)),
        ],
        out_specs=[
            pl.BlockSpec((None, D_INNER, tile), feat),
            pl.BlockSpec((None, tile, D_INNER), tok),
            pl.BlockSpec((None, D_INNER, tile), feat),
            pl.BlockSpec((None, tile, D_INNER), tok),
            pl.BlockSpec((None, None, nblk, D_INNER), lambda b, t: (b, t, 0, 0)),
        ],
        out_shape=[
            jax.ShapeDtypeStruct((B, D_INNER, S), BF16),
            jax.ShapeDtypeStruct((B, S, D_INNER), BF16),
            jax.ShapeDtypeStruct((B, D_INNER, S), BF16),
            jax.ShapeDtypeStruct((B, S, D_INNER), BF16),
            jax.ShapeDtypeStruct((B, S // tile, nblk, D_INNER), F32),
        ],
        compiler_params=pltpu.CompilerParams(
            dimension_semantics=("arbitrary", "arbitrary"), vmem_limit_bytes=VMEM_LIMIT),
        name="moba_proj",
    )(x, norm_g.reshape(1, D_MODEL), wq.T, wk, wv.T, wz, qmul, kmul,
      cos.T, sin.T, jnp.concatenate([cos, cos], axis=1), jnp.concatenate([-sin, sin], axis=1))


PAIRS_PER_STEP = 16
SCORE_LEAD = 4
TILES_PER_STEP = 8


def _pair_tables(n_blocks):
    pairs = [(i, j) for j in range(n_blocks) for i in range(j + 1, n_blocks)]
    return (np.asarray([p[0] for p in pairs], np.int32), np.asarray([p[1] for p in pairs], np.int32))


def _moba_attn_kernel(qi_ref, kj_ref, qT_ref, k_ref, vT_ref, kmean_ref, gate_ref, y_ref,
                      bias_ref, m_ref, l_ref, acc_ref, *, n_blocks, n_pairs):
    def block(i):
        return pl.ds(pl.multiple_of(i * MOBA_BLOCK, MOBA_BLOCK), MOBA_BLOCK)

    km = kmean_ref[...]
    km_hi = km.astype(BF16)
    km_lo = (km - km_hi.astype(F32)).astype(BF16)
    blk = lax.broadcasted_iota(jnp.int32, (n_blocks, MOBA_BLOCK), 0)

    def choose(i, carry):
        qT = qT_ref[:, block(i)]
        gate = (jnp.dot(km_hi, qT, preferred_element_type=F32)
                + jnp.dot(km_lo, qT, preferred_element_type=F32))
        past = blk < i
        cand = jnp.where(past, gate, -jnp.inf)
        chosen = jnp.zeros((n_blocks, MOBA_BLOCK), jnp.bool_)
        for _ in range(MOBA_TOPK):
            top = jnp.max(cand, axis=0, keepdims=True)
            first = jnp.min(jnp.where(cand == top, blk, n_blocks), axis=0, keepdims=True)
            pick = blk == first
            chosen = jnp.logical_or(chosen, jnp.logical_and(pick, past))
            cand = jnp.where(pick, -jnp.inf, cand)
        bias_ref[:, block(i)] = jnp.where(chosen, 0.0, NEG_BIG).astype(F32)
        return carry

    lax.fori_loop(0, n_blocks, choose, 0)

    key_pos = lax.broadcasted_iota(jnp.int32, (MOBA_BLOCK, MOBA_BLOCK), 0)
    qry_pos = lax.broadcasted_iota(jnp.int32, (MOBA_BLOCK, MOBA_BLOCK), 1)
    causal = key_pos <= qry_pos

    def own_step(g, carry):
        s = {}
        for u in range(TILES_PER_STEP + SCORE_LEAD):
            if u < TILES_PER_STEP:
                i = g * TILES_PER_STEP + u
                s[u] = jnp.dot(k_ref[block(i), :], qT_ref[:, block(i)], preferred_element_type=F32)
            if u >= SCORE_LEAD:
                i = g * TILES_PER_STEP + u - SCORE_LEAD
                su = jnp.where(causal, s.pop(u - SCORE_LEAD), NEG_BIG)
                c = jnp.max(su, axis=0, keepdims=True)
                p = jnp.exp2(su - c)
                m_ref[i] = c
                l_ref[i] = jnp.sum(p, axis=0, keepdims=True)
                acc_ref[i] = jnp.dot(vT_ref[:, block(i)], p.astype(BF16), preferred_element_type=F32)
        return carry

    lax.fori_loop(0, n_blocks // TILES_PER_STEP, own_step, 0)

    def pair_step(g, carry):
        ij, s = {}, {}
        for u in range(PAIRS_PER_STEP + SCORE_LEAD):
            if u < PAIRS_PER_STEP:
                idx = g * PAIRS_PER_STEP + u
                ij[u] = i, j = qi_ref[idx], kj_ref[idx]
                s[u] = jnp.dot(k_ref[block(j), :], qT_ref[:, block(i)], preferred_element_type=F32)
            if u >= SCORE_LEAD:
                i, j = ij.pop(u - SCORE_LEAD)
                su = s.pop(u - SCORE_LEAD)
                c = jnp.max(su, axis=0, keepdims=True)
                p = jnp.exp2(su - c)
                r = jnp.sum(p, axis=0, keepdims=True)
                o = jnp.dot(vT_ref[:, block(j)], p.astype(BF16), preferred_element_type=F32)
                cb = c + bias_ref[pl.ds(j, 1), block(i)]
                m_old = m_ref[i]
                m_new = jnp.maximum(m_old, cb)
                a_old = jnp.exp2(m_old - m_new)
                a_new = jnp.exp2(cb - m_new)
                m_ref[i] = m_new
                l_ref[i] = a_old * l_ref[i] + a_new * r
                acc_ref[i] = a_old * acc_ref[i] + a_new * o
        return carry

    lax.fori_loop(0, n_pairs // PAIRS_PER_STEP, pair_step, 0)

    def finish_step(g, carry):
        for u in range(TILES_PER_STEP):
            i = g * TILES_PER_STEP + u
            oT = acc_ref[i] * (1.0 / l_ref[i])
            y_ref[block(i), :] = (oT.T * gate_ref[block(i), :].astype(F32)).astype(BF16)
        return carry

    lax.fori_loop(0, n_blocks // TILES_PER_STEP, finish_step, 0)


def _moba_attn(qT, k, vT, kmean, gate):
    B, S, _ = k.shape
    n_blocks = S // MOBA_BLOCK
    qi, kj = _pair_tables(n_blocks)
    n_pairs = len(qi)
    assert n_pairs % PAIRS_PER_STEP == 0 and n_blocks % TILES_PER_STEP == 0
    grid_spec = pltpu.PrefetchScalarGridSpec(
        num_scalar_prefetch=2,
        grid=(B, N_HEADS),
        in_specs=[
            pl.BlockSpec((None, HEAD_DIM, S), lambda b, h, *_: (b, h, 0)),
            pl.BlockSpec((None, S, HEAD_DIM), lambda b, h, *_: (b, 0, h)),
            pl.BlockSpec((None, HEAD_DIM, S), lambda b, h, *_: (b, h, 0)),
            pl.BlockSpec((None, n_blocks, HEAD_DIM), lambda b, h, *_: (b, 0, h)),
            pl.BlockSpec((None, S, HEAD_DIM), lambda b, h, *_: (b, 0, h)),
        ],
        out_specs=pl.BlockSpec((None, S, HEAD_DIM), lambda b, h, *_: (b, 0, h)),
        scratch_shapes=[
            pltpu.VMEM((n_blocks, S), F32),
            pltpu.VMEM((n_blocks, 1, MOBA_BLOCK), F32),
            pltpu.VMEM((n_blocks, 1, MOBA_BLOCK), F32),
            pltpu.VMEM((n_blocks, HEAD_DIM, MOBA_BLOCK), F32),
        ],
    )
    return pl.pallas_call(
        functools.partial(_moba_attn_kernel, n_blocks=n_blocks, n_pairs=n_pairs),
        grid_spec=grid_spec,
        out_shape=jax.ShapeDtypeStruct((B, S, D_INNER), BF16),
        compiler_params=pltpu.CompilerParams(
            dimension_semantics=("arbitrary", "arbitrary"), vmem_limit_bytes=VMEM_LIMIT),
        name="moba_attn",
    )(jnp.asarray(qi), jnp.asarray(kj), qT, k, vT, kmean, gate)


def _out_proj_kernel(x_ref, y_ref, w_ref, o_ref):
    o_ref[...] = x_ref[...] + jnp.dot(y_ref[...], w_ref[...], preferred_element_type=F32)


def _out_proj(x, y, w_out):
    B, S, _ = x.shape
    rows = B * S
    return pl.pallas_call(
        _out_proj_kernel,
        grid=(rows // OUT_TILE,),
        in_specs=[
            pl.BlockSpec((OUT_TILE, D_MODEL), lambda t: (t, 0)),
            pl.BlockSpec((OUT_TILE, D_INNER), lambda t: (t, 0)),
            _const_spec((D_INNER, D_MODEL)),
        ],
        out_specs=pl.BlockSpec((OUT_TILE, D_MODEL), lambda t: (t, 0)),
        out_shape=jax.ShapeDtypeStruct((rows, D_MODEL), F32),
        compiler_params=pltpu.CompilerParams(
            dimension_semantics=("arbitrary",), vmem_limit_bytes=VMEM_LIMIT),
        name="moba_out_proj",
    )(x.reshape(rows, D_MODEL), y.reshape(rows, D_INNER), w_out.astype(BF16)).reshape(B, S, D_MODEL)


def kernel(x, norm0, w_in0, w_pool0, pool_scale0, w_out0, norm1, w_in1, q_norm1, k_norm1, w_out1):
    B, S, _ = x.shape
    x1 = _pool_layer(x, norm0, w_in0, w_pool0, pool_scale0, w_out0)
    qT, k, vT, gate, kmean = _moba_proj(x1, norm1, w_in1, q_norm1, k_norm1)
    y = _moba_attn(qT, k, vT, kmean.reshape(B, S // MOBA_BLOCK, D_INNER), gate)
    return _out_proj(x1, y, w_out1)
```

```python
import functools
import math

import jax
import jax.numpy as jnp
import numpy as np
from jax import lax
from jax.experimental import pallas as pl
from jax.experimental.pallas import tpu as pltpu

D_MODEL = 1024
D_INNER = 2 * D_MODEL
POOL_WINDOWS = (2, 4, 8, 16)
POOL_GROUP_DIM = D_INNER // len(POOL_WINDOWS)
HEAD_DIM = 128
N_HEADS = D_INNER // HEAD_DIM
MOBA_BLOCK = 256
MOBA_TOPK = 3
ROPE_THETA = 10000.0
EPS = 1e-6

POOL_HALO = 16
NEG_BIG = -1e30
VMEM_LIMIT = 56 * 1024 * 1024
SEQ_TILE = 512
OUT_TILE = 1024

F32 = jnp.float32
BF16 = jnp.bfloat16


def _const_spec(shape):
    nd = len(shape)
    return pl.BlockSpec(shape, lambda *_: (0,) * nd, pipeline_mode=pl.Buffered(1))


def _rms_scale(x):
    return lax.rsqrt(jnp.mean(x * x, axis=-1, keepdims=True) + EPS)


def _silu(z):
    return z / (1.0 + jnp.exp(-z))


def _pool_layer_kernel(x_ref, g_ref, win_ref, wpool_ref, pscale_ref, wout_ref, o_ref, ubuf_ref, *, tile):
    t = pl.program_id(1)
    x = x_ref[...]
    h = (x * _rms_scale(x) * g_ref[...]).astype(BF16)
    uz = jnp.dot(h, win_ref[...], preferred_element_type=F32)
    u = uz[:, :D_INNER]
    z = uz[:, D_INNER:]

    @pl.when(t == 0)
    def _():
        ubuf_ref[0:POOL_HALO, :] = jnp.zeros((POOL_HALO, D_INNER), F32)

    ubuf_ref[POOL_HALO:POOL_HALO + tile, :] = u

    pos = t * tile + lax.broadcasted_iota(jnp.int32, (tile, 1), 0) + 1
    mixed = []
    for g, w in enumerate(POOL_WINDOWS):
        lo, hi = g * POOL_GROUP_DIM, (g + 1) * POOL_GROUP_DIM
        ext = ubuf_ref[:, lo:hi]
        acc, width = ext, 1
        while width < w:
            acc = acc + pltpu.roll(acc, width, 0)
            width *= 2
        inv_cnt = 1.0 / jnp.minimum(pos, w).astype(F32)
        pooled = acc[POOL_HALO:, :] * inv_cnt - u[:, lo:hi]
        mixed.append(jnp.dot(pooled.astype(BF16), wpool_ref[g], preferred_element_type=F32))
    mixed = jnp.concatenate(mixed, axis=-1)

    ubuf_ref[0:POOL_HALO, :] = u[tile - POOL_HALO:, :]

    y = (mixed * pscale_ref[...] * _silu(z)).astype(BF16)
    o_ref[...] = x + jnp.dot(y, wout_ref[...], preferred_element_type=F32)


def _pool_layer(x, norm_g, w_in, w_pool, pool_scale, w_out):
    B, S, _ = x.shape
    tile = SEQ_TILE
    return pl.pallas_call(
        functools.partial(_pool_layer_kernel, tile=tile),
        grid=(B, S // tile),
        in_specs=[
            pl.BlockSpec((None, tile, D_MODEL), lambda b, t: (b, t, 0)),
            _const_spec((1, D_MODEL)),
            _const_spec((D_MODEL, 2 * D_INNER)),
            _const_spec((len(POOL_WINDOWS), POOL_GROUP_DIM, POOL_GROUP_DIM)),
            _const_spec((1, D_INNER)),
            _const_spec((D_INNER, D_MODEL)),
        ],
        out_specs=pl.BlockSpec((None, tile, D_MODEL), lambda b, t: (b, t, 0)),
        out_shape=jax.ShapeDtypeStruct((B, S, D_MODEL), F32),
        scratch_shapes=[pltpu.VMEM((POOL_HALO + tile, D_INNER), F32)],
        compiler_params=pltpu.CompilerParams(
            dimension_semantics=("arbitrary", "arbitrary"), vmem_limit_bytes=VMEM_LIMIT),
        name="pool_layer",
    )(x, norm_g.reshape(1, D_MODEL), w_in.astype(BF16), w_pool.astype(BF16),
      pool_scale.reshape(1, D_INNER), w_out.astype(BF16))


def _moba_proj_kernel(x_ref, g_ref, wqT_ref, wk_ref, wvT_ref, wz_ref, qmul_ref, kmul_ref,
                      cosT_ref, sinT_ref, cos_ref, sin_ref,
                      qT_ref, k_ref, vT_ref, gate_ref, kmean_ref, *, tile):
    x = x_ref[...]
    h = (x * _rms_scale(x) * g_ref[...]).astype(BF16)
    nt = (((1,), (1,)), ((), ()))

    qT = lax.dot_general(wqT_ref[...], h, nt, preferred_element_type=F32)
    cosT, sinT = cosT_ref[...], sinT_ref[...]
    half = HEAD_DIM // 2
    for hd in range(N_HEADS):
        qh = qT[hd * HEAD_DIM:(hd + 1) * HEAD_DIM, :]
        qh = qh * lax.rsqrt(jnp.mean(qh * qh, axis=0, keepdims=True) + EPS) * qmul_ref[...]
        q1, q2 = qh[:half, :], qh[half:, :]
        qT_ref[hd * HEAD_DIM:hd * HEAD_DIM + half, :] = (q1 * cosT - q2 * sinT).astype(BF16)
        qT_ref[hd * HEAD_DIM + half:(hd + 1) * HEAD_DIM, :] = (q2 * cosT + q1 * sinT).astype(BF16)

    k = jnp.dot(h, wk_ref[...], preferred_element_type=F32)
    cos2, sin2 = cos_ref[...], sin_ref[...]
    for hd in range(N_HEADS):
        kh = k[:, hd * HEAD_DIM:(hd + 1) * HEAD_DIM]
        kh = kh * _rms_scale(kh) * kmul_ref[...]
        kh = kh * cos2 + pltpu.roll(kh, half, 1) * sin2
        k_ref[:, hd * HEAD_DIM:(hd + 1) * HEAD_DIM] = kh.astype(BF16)
        for blk in range(tile // MOBA_BLOCK):
            kmean_ref[blk:blk + 1, hd * HEAD_DIM:(hd + 1) * HEAD_DIM] = jnp.mean(
                kh[blk * MOBA_BLOCK:(blk + 1) * MOBA_BLOCK, :], axis=0, keepdims=True)

    vT_ref[...] = lax.dot_general(wvT_ref[...], h, nt, preferred_element_type=F32).astype(BF16)
    gate_ref[...] = _silu(jnp.dot(h, wz_ref[...], preferred_element_type=F32)).astype(BF16)


def _rope_tables(S):
    inv = ROPE_THETA ** (-jnp.arange(0, HEAD_DIM, 2, dtype=F32) / HEAD_DIM)
    ang = jnp.arange(S, dtype=F32)[:, None] * inv[None, :]
    return jnp.cos(ang), jnp.sin(ang)


def _moba_proj(x, norm_g, w_in, q_norm, k_norm):
    B, S, _ = x.shape
    tile = SEQ_TILE
    nblk = tile // MOBA_BLOCK
    wq, wk, wv, wz = jnp.split(w_in.astype(BF16), 4, axis=1)
    cos, sin = _rope_tables(S)
    q_gain = q_norm * (HEAD_DIM ** -0.5 * math.log2(math.e))
    qmul = jnp.broadcast_to(q_gain[:, None], (HEAD_DIM, tile))
    kmul = k_norm.reshape(1, HEAD_DIM)
    tok = lambda b, t: (b, t, 0)
    feat = lambda b, t: (b, 0, t)
    return pl.pallas_call(
        functools.partial(_moba_proj_kernel, tile=tile),
        grid=(B, S // tile),
        in_specs=[
            pl.BlockSpec((None, tile, D_MODEL), tok),
            _const_spec((1, D_MODEL)),
            _const_spec((D_INNER, D_MODEL)),
            _const_spec((D_MODEL, D_INNER)),
            _const_spec((D_INNER, D_MODEL)),
            _const_spec((D_MODEL, D_INNER)),
            _const_spec((HEAD_DIM, tile)),
            _const_spec((1, HEAD_DIM)),
            pl.BlockSpec((HEAD_DIM // 2, tile), lambda b, t: (0, t)),
            pl.BlockSpec((HEAD_DIM // 2, tile), lambda b, t: (0, t)),
            pl.BlockSpec((tile, HEAD_DIM), lambda b, t: (t, 0)),
            pl.BlockSpec((tile, HEAD_DIM), lambda b, t: (t, 0)),
        ],
        out_specs=[
            pl.BlockSpec((None, D_INNER, tile), feat),
            pl.BlockSpec((None, tile, D_INNER), tok),
            pl.BlockSpec((None, D_INNER, tile), feat),
            pl.BlockSpec((None, tile, D_INNER), tok),
            pl.BlockSpec((None, None, nblk, D_INNER), lambda b, t: (b, t, 0, 0)),
        ],
        out_shape=[
            jax.ShapeDtypeStruct((B, D_INNER, S), BF16),
            jax.ShapeDtypeStruct((B, S, D_INNER), BF16),
            jax.ShapeDtypeStruct((B, D_INNER, S), BF16),
            jax.ShapeDtypeStruct((B, S, D_INNER), BF16),
            jax.ShapeDtypeStruct((B, S // tile, nblk, D_INNER), F32),
        ],
        compiler_params=pltpu.CompilerParams(
            dimension_semantics=("arbitrary", "arbitrary"), vmem_limit_bytes=VMEM_LIMIT),
        name="moba_proj",
    )(x, norm_g.reshape(1, D_MODEL), wq.T, wk, wv.T, wz, qmul, kmul,
      cos.T, sin.T, jnp.concatenate([cos, cos], axis=1), jnp.concatenate([-sin, sin], axis=1))


CHAINS_PER_STEP = 16
SCORE_LEAD = 3
TILES_PER_STEP = 8
CHOOSE_PER_STEP = 4
ONES_ROWS = 16


def _chain_tables(n_blocks):
    chains = []
    for i in range(1, n_blocks):
        js = list(range(i)) + ([i] if i % 2 else [])
        chains += [(i, js[n], js[n + 1]) for n in range(0, len(js), 2)]
    chains.sort(key=lambda c: (c[1], c[0]))
    return tuple(np.asarray([c[n] for c in chains], np.int32) for n in range(3))


def _moba_attn_kernel(qi_ref, ka_ref, kb_ref, qT_ref, k_ref, vT_ref, kmean_ref, gate_ref, y_ref,
                      bias_ref, m_ref, l_ref, acc_ref, *, n_blocks, n_chains):
    def block(i):
        return pl.ds(pl.multiple_of(i * MOBA_BLOCK, MOBA_BLOCK), MOBA_BLOCK)

    km = kmean_ref[...]
    km_hi = km.astype(BF16)
    km_lo = (km - km_hi.astype(F32)).astype(BF16)
    blk = lax.broadcasted_iota(jnp.int32, (n_blocks, MOBA_BLOCK), 0)

    def choose_step(g, carry):
        for u in range(CHOOSE_PER_STEP):
            i = g * CHOOSE_PER_STEP + u
            qT = qT_ref[:, block(i)]
            gate = (jnp.dot(km_hi, qT, preferred_element_type=F32)
                    + jnp.dot(km_lo, qT, preferred_element_type=F32))
            past = blk < i
            cand = jnp.where(past, gate, -jnp.inf)
            chosen = jnp.zeros((n_blocks, MOBA_BLOCK), jnp.bool_)
            for _ in range(MOBA_TOPK):
                top = jnp.max(cand, axis=0, keepdims=True)
                first = jnp.min(jnp.where(cand == top, blk, n_blocks), axis=0, keepdims=True)
                pick = blk == first
                chosen = jnp.logical_or(chosen, jnp.logical_and(pick, past))
                cand = jnp.where(pick, -jnp.inf, cand)
            bias_ref[:, block(i)] = jnp.where(chosen, 0.0, NEG_BIG).astype(F32)
        return carry

    lax.fori_loop(0, n_blocks // CHOOSE_PER_STEP, choose_step, 0)

    key_pos = lax.broadcasted_iota(jnp.int32, (MOBA_BLOCK, MOBA_BLOCK), 0)
    qry_pos = lax.broadcasted_iota(jnp.int32, (MOBA_BLOCK, MOBA_BLOCK), 1)
    causal = key_pos <= qry_pos

    def own_step(g, carry):
        s = {}
        for u in range(TILES_PER_STEP + SCORE_LEAD):
            if u < TILES_PER_STEP:
                i = g * TILES_PER_STEP + u
                s[u] = jnp.dot(k_ref[block(i), :], qT_ref[:, block(i)], preferred_element_type=F32)
            if u >= SCORE_LEAD:
                i = g * TILES_PER_STEP + u - SCORE_LEAD
                su = jnp.where(causal, s.pop(u - SCORE_LEAD), NEG_BIG)
                c = jnp.max(su, axis=0, keepdims=True)
                p = jnp.exp2(su - c)
                m_ref[i] = c
                l_ref[i] = jnp.sum(p, axis=0, keepdims=True)
                acc_ref[i] = jnp.dot(vT_ref[:, block(i)], p.astype(BF16), preferred_element_type=F32)
        return carry

    lax.fori_loop(0, n_blocks // TILES_PER_STEP, own_step, 0)

    ones = jnp.ones((ONES_ROWS, 2 * MOBA_BLOCK), BF16)

    def chain_step(g, carry):
        ids, s = {}, {}
        for u in range(CHAINS_PER_STEP + SCORE_LEAD):
            if u < CHAINS_PER_STEP:
                idx = g * CHAINS_PER_STEP + u
                ids[u] = i, ja, jb = qi_ref[idx], ka_ref[idx], kb_ref[idx]
                qT = qT_ref[:, block(i)]
                s[u] = (jnp.dot(k_ref[block(ja), :], qT, preferred_element_type=F32),
                        jnp.dot(k_ref[block(jb), :], qT, preferred_element_type=F32))
            if u >= SCORE_LEAD:
                i, ja, jb = ids.pop(u - SCORE_LEAD)
                sa, sb = s.pop(u - SCORE_LEAD)
                ba = bias_ref[pl.ds(ja, 1), block(i)]
                bb = bias_ref[pl.ds(jb, 1), block(i)]
                m_old = m_ref[i]
                m_new = jnp.maximum(m_old, jnp.maximum(jnp.max(sa, axis=0, keepdims=True) + ba,
                                                       jnp.max(sb, axis=0, keepdims=True) + bb))
                p = jnp.concatenate([jnp.exp2(sa - (m_new - ba)).astype(BF16),
                                     jnp.exp2(sb - (m_new - bb)).astype(BF16)], axis=0)
                v_aug = jnp.concatenate(
                    [jnp.concatenate([vT_ref[:, block(ja)], vT_ref[:, block(jb)]], axis=1), ones], axis=0)
                o_aug = jnp.dot(v_aug, p, preferred_element_type=F32)
                a_old = jnp.exp2(m_old - m_new)
                m_ref[i] = m_new
                l_ref[i] = a_old * l_ref[i] + o_aug[HEAD_DIM:HEAD_DIM + 1, :]
                acc_ref[i] = a_old * acc_ref[i] + o_aug[:HEAD_DIM, :]
        return carry

    lax.fori_loop(0, n_chains // CHAINS_PER_STEP, chain_step, 0)

    def finish_step(g, carry):
        for u in range(TILES_PER_STEP):
            i = g * TILES_PER_STEP + u
            oT = acc_ref[i] * (1.0 / l_ref[i])
            y_ref[block(i), :] = (oT.T * gate_ref[block(i), :].astype(F32)).astype(BF16)
        return carry

    lax.fori_loop(0, n_blocks // TILES_PER_STEP, finish_step, 0)


def _moba_attn(qT, k, vT, kmean, gate):
    B, S, _ = k.shape
    n_blocks = S // MOBA_BLOCK
    tables = _chain_tables(n_blocks)
    n_chains = len(tables[0])
    assert n_chains % CHAINS_PER_STEP == 0
    assert n_blocks % TILES_PER_STEP == 0 and n_blocks % CHOOSE_PER_STEP == 0
    grid_spec = pltpu.PrefetchScalarGridSpec(
        num_scalar_prefetch=3,
        grid=(B, N_HEADS),
        in_specs=[
            pl.BlockSpec((None, HEAD_DIM, S), lambda b, h, *_: (b, h, 0)),
            pl.BlockSpec((None, S, HEAD_DIM), lambda b, h, *_: (b, 0, h)),
            pl.BlockSpec((None, HEAD_DIM, S), lambda b, h, *_: (b, h, 0)),
            pl.BlockSpec((None, n_blocks, HEAD_DIM), lambda b, h, *_: (b, 0, h)),
            pl.BlockSpec((None, S, HEAD_DIM), lambda b, h, *_: (b, 0, h)),
        ],
        out_specs=pl.BlockSpec((None, S, HEAD_DIM), lambda b, h, *_: (b, 0, h)),
        scratch_shapes=[
            pltpu.VMEM((n_blocks, S), F32),
            pltpu.VMEM((n_blocks, 1, MOBA_BLOCK), F32),
            pltpu.VMEM((n_blocks, 1, MOBA_BLOCK), F32),
            pltpu.VMEM((n_blocks, HEAD_DIM, MOBA_BLOCK), F32),
        ],
    )
    return pl.pallas_call(
        functools.partial(_moba_attn_kernel, n_blocks=n_blocks, n_chains=n_chains),
        grid_spec=grid_spec,
        out_shape=jax.ShapeDtypeStruct((B, S, D_INNER), BF16),
        compiler_params=pltpu.CompilerParams(
            dimension_semantics=("arbitrary", "arbitrary"), vmem_limit_bytes=VMEM_LIMIT),
        name="moba_attn",
    )(*(jnp.asarray(t) for t in tables), qT, k, vT, kmean, gate)


def _out_proj_kernel(x_ref, y_ref, w_ref, o_ref):
    o_ref[...] = x_ref[...] + jnp.dot(y_ref[...], w_ref[...], preferred_element_type=F32)


def _out_proj(x, y, w_out):
    B, S, _ = x.shape
    rows = B * S
    return pl.pallas_call(
        _out_proj_kernel,
        grid=(rows // OUT_TILE,),
        in_specs=[
            pl.BlockSpec((OUT_TILE, D_MODEL), lambda t: (t, 0)),
            pl.BlockSpec((OUT_TILE, D_INNER), lambda t: (t, 0)),
            _const_spec((D_INNER, D_MODEL)),
        ],
        out_specs=pl.BlockSpec((OUT_TILE, D_MODEL), lambda t: (t, 0)),
        out_shape=jax.ShapeDtypeStruct((rows, D_MODEL), F32),
        compiler_params=pltpu.CompilerParams(
            dimension_semantics=("arbitrary",), vmem_limit_bytes=VMEM_LIMIT),
        name="moba_out_proj",
    )(x.reshape(rows, D_MODEL), y.reshape(rows, D_INNER), w_out.astype(BF16)).reshape(B, S, D_MODEL)


def kernel(x, norm0, w_in0, w_pool0, pool_scale0, w_out0, norm1, w_in1, q_norm1, k_norm1, w_out1):
    B, S, _ = x.shape
    x1 = _pool_layer(x, norm0, w_in0, w_pool0, pool_scale0, w_out0)
    qT, k, vT, gate, kmean = _moba_proj(x1, norm1, w_in1, q_norm1, k_norm1)
    y = _moba_attn(qT, k, vT, kmean.reshape(B, S // MOBA_BLOCK, D_INNER), gate)
    return _out_proj(x1, y, w_out1)
```

```python
import functools
import math

import jax
import jax.numpy as jnp
import numpy as np
from jax import lax
from jax.experimental import pallas as pl
from jax.experimental.pallas import tpu as pltpu

D_MODEL = 1024
D_INNER = 2 * D_MODEL
POOL_WINDOWS = (2, 4, 8, 16)
POOL_GROUP_DIM = D_INNER // len(POOL_WINDOWS)
HEAD_DIM = 128
N_HEADS = D_INNER // HEAD_DIM
MOBA_BLOCK = 256
MOBA_TOPK = 3
ROPE_THETA = 10000.0
EPS = 1e-6

POOL_HALO = 16
NEG_BIG = -1e30
VMEM_LIMIT = 56 * 1024 * 1024
SEQ_TILE = 512
OUT_TILE = 1024

F32 = jnp.float32
BF16 = jnp.bfloat16


def _const_spec(shape):
    nd = len(shape)
    return pl.BlockSpec(shape, lambda *_: (0,) * nd, pipeline_mode=pl.Buffered(1))


def _rms_scale(x):
    return lax.rsqrt(jnp.mean(x * x, axis=-1, keepdims=True) + EPS)


def _silu(z):
    return z / (1.0 + jnp.exp(-z))


def _pool_layer_kernel(x_ref, g_ref, win_ref, wpool_ref, pscale_ref, wout_ref, o_ref, ubuf_ref, *, tile):
    t = pl.program_id(1)
    x = x_ref[...]
    h = (x * _rms_scale(x) * g_ref[...]).astype(BF16)
    uz = jnp.dot(h, win_ref[...], preferred_element_type=F32)
    u = uz[:, :D_INNER]
    z = uz[:, D_INNER:]

    @pl.when(t == 0)
    def _():
        ubuf_ref[0:POOL_HALO, :] = jnp.zeros((POOL_HALO, D_INNER), F32)

    ubuf_ref[POOL_HALO:POOL_HALO + tile, :] = u

    pos = t * tile + lax.broadcasted_iota(jnp.int32, (tile, 1), 0) + 1
    mixed = []
    for g, w in enumerate(POOL_WINDOWS):
        lo, hi = g * POOL_GROUP_DIM, (g + 1) * POOL_GROUP_DIM
        ext = ubuf_ref[:, lo:hi]
        acc, width = ext, 1
        while width < w:
            acc = acc + pltpu.roll(acc, width, 0)
            width *= 2
        inv_cnt = 1.0 / jnp.minimum(pos, w).astype(F32)
        pooled = acc[POOL_HALO:, :] * inv_cnt - u[:, lo:hi]
        mixed.append(jnp.dot(pooled.astype(BF16), wpool_ref[g], preferred_element_type=F32))
    mixed = jnp.concatenate(mixed, axis=-1)

    ubuf_ref[0:POOL_HALO, :] = u[tile - POOL_HALO:, :]

    y = (mixed * pscale_ref[...] * _silu(z)).astype(BF16)
    o_ref[...] = x + jnp.dot(y, wout_ref[...], preferred_element_type=F32)


def _pool_layer(x, norm_g, w_in, w_pool, pool_scale, w_out):
    B, S, _ = x.shape
    tile = SEQ_TILE
    return pl.pallas_call(
        functools.partial(_pool_layer_kernel, tile=tile),
        grid=(B, S // tile),
        in_specs=[
            pl.BlockSpec((None, tile, D_MODEL), lambda b, t: (b, t, 0)),
            _const_spec((1, D_MODEL)),
            _const_spec((D_MODEL, 2 * D_INNER)),
            _const_spec((len(POOL_WINDOWS), POOL_GROUP_DIM, POOL_GROUP_DIM)),
            _const_spec((1, D_INNER)),
            _const_spec((D_INNER, D_MODEL)),
        ],
        out_specs=pl.BlockSpec((None, tile, D_MODEL), lambda b, t: (b, t, 0)),
        out_shape=jax.ShapeDtypeStruct((B, S, D_MODEL), F32),
        scratch_shapes=[pltpu.VMEM((POOL_HALO + tile, D_INNER), F32)],
        compiler_params=pltpu.CompilerParams(
            dimension_semantics=("arbitrary", "arbitrary"), vmem_limit_bytes=VMEM_LIMIT),
        name="pool_layer",
    )(x, norm_g.reshape(1, D_MODEL), w_in.astype(BF16), w_pool.astype(BF16),
      pool_scale.reshape(1, D_INNER), w_out.astype(BF16))


def _moba_proj_kernel(x_ref, g_ref, wqT_ref, wk_ref, wvT_ref, wz_ref, qmul_ref, kmul_ref,
                      cosT_ref, sinT_ref, cos_ref, sin_ref,
                      qT_ref, k_ref, vT_ref, gate_ref, kmean_ref, *, tile):
    x = x_ref[...]
    h = (x * _rms_scale(x) * g_ref[...]).astype(BF16)
    nt = (((1,), (1,)), ((), ()))

    qT = lax.dot_general(wqT_ref[...], h, nt, preferred_element_type=F32)
    cosT, sinT = cosT_ref[...], sinT_ref[...]
    half = HEAD_DIM // 2
    for hd in range(N_HEADS):
        qh = qT[hd * HEAD_DIM:(hd + 1) * HEAD_DIM, :]
        qh = qh * lax.rsqrt(jnp.mean(qh * qh, axis=0, keepdims=True) + EPS) * qmul_ref[...]
        q1, q2 = qh[:half, :], qh[half:, :]
        qT_ref[hd * HEAD_DIM:hd * HEAD_DIM + half, :] = (q1 * cosT - q2 * sinT).astype(BF16)
        qT_ref[hd * HEAD_DIM + half:(hd + 1) * HEAD_DIM, :] = (q2 * cosT + q1 * sinT).astype(BF16)

    k = jnp.dot(h, wk_ref[...], preferred_element_type=F32)
    cos2, sin2 = cos_ref[...], sin_ref[...]
    for hd in range(N_HEADS):
        kh = k[:, hd * HEAD_DIM:(hd + 1) * HEAD_DIM]
        kh = kh * _rms_scale(kh) * kmul_ref[...]
        kh = kh * cos2 + pltpu.roll(kh, half, 1) * sin2
        k_ref[:, hd * HEAD_DIM:(hd + 1) * HEAD_DIM] = kh.astype(BF16)
        for blk in range(tile // MOBA_BLOCK):
            kmean_ref[blk:blk + 1, hd * HEAD_DIM:(hd + 1) * HEAD_DIM] = jnp.mean(
                kh[blk * MOBA_BLOCK:(blk + 1) * MOBA_BLOCK, :], axis=0, keepdims=True)

    vT_ref[...] = lax.dot_general(wvT_ref[...], h, nt, preferred_element_type=F32).astype(BF16)
    gate_ref[...] = _silu(jnp.dot(h, wz_ref[...], preferred_element_type=F32)).astype(BF16)


def _rope_tables(S):
    inv = ROPE_THETA ** (-jnp.arange(0, HEAD_DIM, 2, dtype=F32) / HEAD_DIM)
    ang = jnp.arange(S, dtype=F32)[:, None] * inv[None, :]
    return jnp.cos(ang), jnp.sin(ang)


def _moba_proj(x, norm_g, w_in, q_norm, k_norm):
    B, S, _ = x.shape
    tile = SEQ_TILE
    nblk = tile // MOBA_BLOCK
    wq, wk, wv, wz = jnp.split(w_in.astype(BF16), 4, axis=1)
    cos, sin = _rope_tables(S)
    q_gain = q_norm * (HEAD_DIM ** -0.5 * math.log2(math.e))
    qmul = jnp.broadcast_to(q_gain[:, None], (HEAD_DIM, tile))
    kmul = k_norm.reshape(1, HEAD_DIM)
    tok = lambda b, t: (b, t, 0)
    feat = lambda b, t: (b, 0, t)
    return pl.pallas_call(
        functools.partial(_moba_proj_kernel, tile=tile),
        grid=(B, S // tile),
        in_specs=[
            pl.BlockSpec((None, tile, D_MODEL), tok),
            _const_spec((1, D_MODEL)),
            _const_spec((D_INNER, D_MODEL)),
            _const_spec((D_MODEL, D_INNER)),
            _const_spec((D_INNER, D_MODEL)),
            _const_spec((D_MODEL, D_INNER)),
            _const_spec((HEAD_DIM, tile)),
            _const_spec((1, HEAD_DIM)),
            pl.BlockSpec((HEAD_DIM // 2, tile), lambda b, t: (0, t)),
            pl.BlockSpec((HEAD_DIM // 2, tile), lambda b, t: (0, t)),
            pl.BlockSpec((tile, HEAD_DIM), lambda b, t: (t, 0)),
            pl.BlockSpec((tile, HEAD_DIM), lambda b, t: (t, 0)),
        ],
        out_specs=[
            pl.BlockSpec((None, D_INNER, tile), feat),
            pl.BlockSpec((None, tile, D_INNER), tok),
            pl.BlockSpec((None, D_INNER, tile), feat),
            pl.BlockSpec((None, tile, D_INNER), tok),
            pl.BlockSpec((None, None, nblk, D_INNER), lambda b, t: (b, t, 0, 0)),
        ],
        out_shape=[
            jax.ShapeDtypeStruct((B, D_INNER, S), BF16),
            jax.ShapeDtypeStruct((B, S, D_INNER), BF16),
            jax.ShapeDtypeStruct((B, D_INNER, S), BF16),
            jax.ShapeDtypeStruct((B, S, D_INNER), BF16),
            jax.ShapeDtypeStruct((B, S // tile, nblk, D_INNER), F32),
        ],
        compiler_params=pltpu.CompilerParams(
            dimension_semantics=("arbitrary", "arbitrary"), vmem_limit_bytes=VMEM_LIMIT),
        name="moba_proj",
    )(x, norm_g.reshape(1, D_MODEL), wq.T, wk, wv.T, wz, qmul, kmul,
      cos.T, sin.T, jnp.concatenate([cos, cos], axis=1), jnp.concatenate([-sin, sin], axis=1))


CHAINS_PER_STEP = 32
SCORE_LEAD = 2
OWN_LEAD = 3
TILES_PER_STEP = 8
CHOOSE_PER_STEP = 4
ONES_ROWS = 16


def _chain_tables(n_blocks):
    chains = []
    for i in range(1, n_blocks):
        js = list(range(i)) + ([i] if i % 2 else [])
        chains += [(i, js[n], js[n + 1]) for n in range(0, len(js), 2)]
    chains.sort(key=lambda c: (c[1], c[0]))
    return tuple(np.asarray([c[n] for c in chains], np.int32) for n in range(3))


def _moba_attn_kernel(qi_ref, ka_ref, kb_ref, zero_ref, qT_ref, k_ref, vT_ref, kmean_ref, gate_ref, y_ref,
                      bias_ref, m_ref, l_ref, acc_ref, *x_refs, n_blocks, n_chains):
    def block(i):
        return pl.ds(pl.multiple_of(i * MOBA_BLOCK, MOBA_BLOCK), MOBA_BLOCK)

    km = kmean_ref[...]
    km_hi = km.astype(BF16)
    km_lo = (km - km_hi.astype(F32)).astype(BF16)
    blk = lax.broadcasted_iota(jnp.int32, (n_blocks, MOBA_BLOCK), 0)

    def choose_step(g, carry):
        for u in range(CHOOSE_PER_STEP):
            i = g * CHOOSE_PER_STEP + u
            qT = qT_ref[:, block(i)]
            gate = (jnp.dot(km_hi, qT, preferred_element_type=F32)
                    + jnp.dot(km_lo, qT, preferred_element_type=F32))
            past = blk < i
            cand = jnp.where(past, gate, -jnp.inf)
            chosen = jnp.zeros((n_blocks, MOBA_BLOCK), jnp.bool_)
            for _ in range(MOBA_TOPK):
                top = jnp.max(cand, axis=0, keepdims=True)
                first = jnp.min(jnp.where(cand == top, blk, n_blocks), axis=0, keepdims=True)
                pick = blk == first
                chosen = jnp.logical_or(chosen, jnp.logical_and(pick, past))
                cand = jnp.where(pick, -jnp.inf, cand)
            bias_ref[:, block(i)] = jnp.where(chosen, 0.0, NEG_BIG).astype(F32)
        return carry

    lax.fori_loop(0, n_blocks // CHOOSE_PER_STEP, choose_step, 0)

    key_pos = lax.broadcasted_iota(jnp.int32, (MOBA_BLOCK, MOBA_BLOCK), 0)
    qry_pos = lax.broadcasted_iota(jnp.int32, (MOBA_BLOCK, MOBA_BLOCK), 1)
    causal = key_pos <= qry_pos

    def own_step(g, carry):
        s = {}
        for u in range(TILES_PER_STEP + OWN_LEAD):
            if u < TILES_PER_STEP:
                i = g * TILES_PER_STEP + u
                s[u] = jnp.dot(k_ref[block(i), :], qT_ref[:, block(i)], preferred_element_type=F32)
            if u >= OWN_LEAD:
                i = g * TILES_PER_STEP + u - OWN_LEAD
                su = jnp.where(causal, s.pop(u - OWN_LEAD), NEG_BIG)
                c = jnp.max(su, axis=0, keepdims=True)
                p = jnp.exp2(su - c)
                m_ref[i] = c
                l_ref[i] = jnp.sum(p, axis=0, keepdims=True)
                acc_ref[i] = jnp.dot(vT_ref[:, block(i)], p.astype(BF16), preferred_element_type=F32)
        return carry

    lax.fori_loop(0, n_blocks // TILES_PER_STEP, own_step, 0)

    ones = jnp.ones((ONES_ROWS, 2 * MOBA_BLOCK), BF16)

    def chain_step(g, carry):
        ids, s = {}, {}
        for u in range(CHAINS_PER_STEP + SCORE_LEAD):
            if u < CHAINS_PER_STEP:
                idx = g * CHAINS_PER_STEP + u
                ids[u] = i, ja, jb = qi_ref[idx], ka_ref[idx], kb_ref[idx]
                qT = qT_ref[:, block(i)]
                m0 = m_ref[i]
                xa = jnp.dot(k_ref[block(ja), :], qT, preferred_element_type=F32) - (
                    m0 - bias_ref[pl.ds(ja, 1), block(i)])
                xb = jnp.dot(k_ref[block(jb), :], qT, preferred_element_type=F32) - (
                    m0 - bias_ref[pl.ds(jb, 1), block(i)])
                top = jnp.maximum(jnp.max(xa, axis=0, keepdims=True), jnp.max(xb, axis=0, keepdims=True))
                x_ref = x_refs[u % len(x_refs)]
                x_ref[zero_ref[0], :MOBA_BLOCK, :] = xa.astype(BF16)
                x_ref[zero_ref[0], MOBA_BLOCK:, :] = xb.astype(BF16)
                s[u] = (m0, top)
            if u >= SCORE_LEAD:
                i, ja, jb = ids.pop(u - SCORE_LEAD)
                m0, top = s.pop(u - SCORE_LEAD)
                m_old = m_ref[i]
                shift = jnp.maximum(m_old - m0, top).astype(BF16)
                m_new = m0 + shift.astype(F32)
                p = jnp.exp2(x_refs[(u - SCORE_LEAD) % len(x_refs)][zero_ref[0]] - shift)
                v_aug = jnp.concatenate(
                    [jnp.concatenate([vT_ref[:, block(ja)], vT_ref[:, block(jb)]], axis=1), ones], axis=0)
                o_aug = jnp.dot(v_aug, p, preferred_element_type=F32)
                a_old = jnp.exp2(m_old - m_new)
                m_ref[i] = m_new
                l_ref[i] = a_old * l_ref[i] + o_aug[HEAD_DIM:HEAD_DIM + 1, :]
                acc_ref[i] = a_old * acc_ref[i] + o_aug[:HEAD_DIM, :]
        return carry

    lax.fori_loop(0, n_chains // CHAINS_PER_STEP, chain_step, 0)

    def finish_step(g, carry):
        for u in range(TILES_PER_STEP):
            i = g * TILES_PER_STEP + u
            oT = acc_ref[i] * (1.0 / l_ref[i])
            y_ref[block(i), :] = (oT.T * gate_ref[block(i), :].astype(F32)).astype(BF16)
        return carry

    lax.fori_loop(0, n_blocks // TILES_PER_STEP, finish_step, 0)


def _moba_attn(qT, k, vT, kmean, gate):
    B, S, _ = k.shape
    n_blocks = S // MOBA_BLOCK
    tables = _chain_tables(n_blocks)
    n_chains = len(tables[0])
    assert n_chains % CHAINS_PER_STEP == 0
    assert n_blocks % TILES_PER_STEP == 0 and n_blocks % CHOOSE_PER_STEP == 0
    grid_spec = pltpu.PrefetchScalarGridSpec(
        num_scalar_prefetch=4,
        grid=(B, N_HEADS),
        in_specs=[
            pl.BlockSpec((None, HEAD_DIM, S), lambda b, h, *_: (b, h, 0)),
            pl.BlockSpec((None, S, HEAD_DIM), lambda b, h, *_: (b, 0, h)),
            pl.BlockSpec((None, HEAD_DIM, S), lambda b, h, *_: (b, h, 0)),
            pl.BlockSpec((None, n_blocks, HEAD_DIM), lambda b, h, *_: (b, 0, h)),
            pl.BlockSpec((None, S, HEAD_DIM), lambda b, h, *_: (b, 0, h)),
        ],
        out_specs=pl.BlockSpec((None, S, HEAD_DIM), lambda b, h, *_: (b, 0, h)),
        scratch_shapes=[
            pltpu.VMEM((n_blocks, S), F32),
            pltpu.VMEM((n_blocks, 1, MOBA_BLOCK), F32),
            pltpu.VMEM((n_blocks, 1, MOBA_BLOCK), F32),
            pltpu.VMEM((n_blocks, HEAD_DIM, MOBA_BLOCK), F32),
        ] + [pltpu.VMEM((1, 2 * MOBA_BLOCK, MOBA_BLOCK), BF16)] * (SCORE_LEAD + 1),
    )
    return pl.pallas_call(
        functools.partial(_moba_attn_kernel, n_blocks=n_blocks, n_chains=n_chains),
        grid_spec=grid_spec,
        out_shape=jax.ShapeDtypeStruct((B, S, D_INNER), BF16),
        compiler_params=pltpu.CompilerParams(
            dimension_semantics=("arbitrary", "arbitrary"), vmem_limit_bytes=VMEM_LIMIT),
        name="moba_attn",
    )(*(jnp.asarray(t) for t in tables), jnp.zeros((1,), jnp.int32), qT, k, vT, kmean, gate)


def _out_proj_kernel(x_ref, y_ref, w_ref, o_ref):
    o_ref[...] = x_ref[...] + jnp.dot(y_ref[...], w_ref[...], preferred_element_type=F32)


def _out_proj(x, y, w_out):
    B, S, _ = x.shape
    rows = B * S
    return pl.pallas_call(
        _out_proj_kernel,
        grid=(rows // OUT_TILE,),
        in_specs=[
            pl.BlockSpec((OUT_TILE, D_MODEL), lambda t: (t, 0)),
            pl.BlockSpec((OUT_TILE, D_INNER), lambda t: (t, 0)),
            _const_spec((D_INNER, D_MODEL)),
        ],
        out_specs=pl.BlockSpec((OUT_TILE, D_MODEL), lambda t: (t, 0)),
        out_shape=jax.ShapeDtypeStruct((rows, D_MODEL), F32),
        compiler_params=pltpu.CompilerParams(
            dimension_semantics=("arbitrary",), vmem_limit_bytes=VMEM_LIMIT),
        name="moba_out_proj",
    )(x.reshape(rows, D_MODEL), y.reshape(rows, D_INNER), w_out.astype(BF16)).reshape(B, S, D_MODEL)


def kernel(x, norm0, w_in0, w_pool0, pool_scale0, w_out0, norm1, w_in1, q_norm1, k_norm1, w_out1):
    B, S, _ = x.shape
    x1 = _pool_layer(x, norm0, w_in0, w_pool0, pool_scale0, w_out0)
    qT, k, vT, gate, kmean = _moba_proj(x1, norm1, w_in1, q_norm1, k_norm1)
    y = _moba_attn(qT, k, vT, kmean.reshape(B, S // MOBA_BLOCK, D_INNER), gate)
    return _out_proj(x1, y, w_out1)
```

```python
import functools
import math

import jax
import jax.numpy as jnp
import numpy as np
from jax import lax
from jax.experimental import pallas as pl
from jax.experimental.pallas import tpu as pltpu

D_MODEL = 1024
D_INNER = 2 * D_MODEL
POOL_WINDOWS = (2, 4, 8, 16)
POOL_GROUP_DIM = D_INNER // len(POOL_WINDOWS)
HEAD_DIM = 128
N_HEADS = D_INNER // HEAD_DIM
MOBA_BLOCK = 256
MOBA_TOPK = 3
ROPE_THETA = 10000.0
EPS = 1e-6

POOL_HALO = 16
NEG_BIG = -1e30
VMEM_LIMIT = 56 * 1024 * 1024
SEQ_TILE = 512
OUT_TILE = 1024

F32 = jnp.float32
BF16 = jnp.bfloat16


def _const_spec(shape):
    nd = len(shape)
    return pl.BlockSpec(shape, lambda *_: (0,) * nd, pipeline_mode=pl.Buffered(1))


def _rms_scale(x):
    return lax.rsqrt(jnp.mean(x * x, axis=-1, keepdims=True) + EPS)


def _silu(z):
    return z / (1.0 + jnp.exp(-z))


def _pool_layer_kernel(x_ref, g_ref, win_ref, wpool_ref, pscale_ref, wout_ref, o_ref, ubuf_ref, *, tile):
    t = pl.program_id(1)
    x = x_ref[...]
    h = (x * _rms_scale(x) * g_ref[...]).astype(BF16)
    uz = jnp.dot(h, win_ref[...], preferred_element_type=F32)
    u = uz[:, :D_INNER]
    z = uz[:, D_INNER:]

    @pl.when(t == 0)
    def _():
        ubuf_ref[0:POOL_HALO, :] = jnp.zeros((POOL_HALO, D_INNER), F32)

    ubuf_ref[POOL_HALO:POOL_HALO + tile, :] = u

    pos = t * tile + lax.broadcasted_iota(jnp.int32, (tile, 1), 0) + 1
    mixed = []
    for g, w in enumerate(POOL_WINDOWS):
        lo, hi = g * POOL_GROUP_DIM, (g + 1) * POOL_GROUP_DIM
        ext = ubuf_ref[:, lo:hi]
        acc, width = ext, 1
        while width < w:
            acc = acc + pltpu.roll(acc, width, 0)
            width *= 2
        inv_cnt = 1.0 / jnp.minimum(pos, w).astype(F32)
        pooled = acc[POOL_HALO:, :] * inv_cnt - u[:, lo:hi]
        mixed.append(jnp.dot(pooled.astype(BF16), wpool_ref[g], preferred_element_type=F32))
    mixed = jnp.concatenate(mixed, axis=-1)

    ubuf_ref[0:POOL_HALO, :] = u[tile - POOL_HALO:, :]

    y = (mixed * pscale_ref[...] * _silu(z)).astype(BF16)
    o_ref[...] = x + jnp.dot(y, wout_ref[...], preferred_element_type=F32)


def _pool_layer(x, norm_g, w_in, w_pool, pool_scale, w_out):
    B, S, _ = x.shape
    tile = SEQ_TILE
    return pl.pallas_call(
        functools.partial(_pool_layer_kernel, tile=tile),
        grid=(B, S // tile),
        in_specs=[
            pl.BlockSpec((None, tile, D_MODEL), lambda b, t: (b, t, 0)),
            _const_spec((1, D_MODEL)),
            _const_spec((D_MODEL, 2 * D_INNER)),
            _const_spec((len(POOL_WINDOWS), POOL_GROUP_DIM, POOL_GROUP_DIM)),
            _const_spec((1, D_INNER)),
            _const_spec((D_INNER, D_MODEL)),
        ],
        out_specs=pl.BlockSpec((None, tile, D_MODEL), lambda b, t: (b, t, 0)),
        out_shape=jax.ShapeDtypeStruct((B, S, D_MODEL), F32),
        scratch_shapes=[pltpu.VMEM((POOL_HALO + tile, D_INNER), F32)],
        compiler_params=pltpu.CompilerParams(
            dimension_semantics=("arbitrary", "arbitrary"), vmem_limit_bytes=VMEM_LIMIT),
        name="pool_layer",
    )(x, norm_g.reshape(1, D_MODEL), w_in.astype(BF16), w_pool.astype(BF16),
      pool_scale.reshape(1, D_INNER), w_out.astype(BF16))


def _moba_proj_kernel(x_ref, g_ref, wqT_ref, wk_ref, wvT_ref, wz_ref, qmul_ref, kmul_ref,
                      cosT_ref, sinT_ref, cos_ref, sin_ref,
                      qT_ref, k_ref, vT_ref, gate_ref, kmean_ref, *, tile):
    x = x_ref[...]
    h = (x * _rms_scale(x) * g_ref[...]).astype(BF16)
    nt = (((1,), (1,)), ((), ()))

    qT = lax.dot_general(wqT_ref[...], h, nt, preferred_element_type=F32)
    cosT, sinT = cosT_ref[...], sinT_ref[...]
    half = HEAD_DIM // 2
    for hd in range(N_HEADS):
        qh = qT[hd * HEAD_DIM:(hd + 1) * HEAD_DIM, :]
        qh = qh * lax.rsqrt(jnp.mean(qh * qh, axis=0, keepdims=True) + EPS) * qmul_ref[...]
        q1, q2 = qh[:half, :], qh[half:, :]
        qT_ref[hd * HEAD_DIM:hd * HEAD_DIM + half, :] = (q1 * cosT - q2 * sinT).astype(BF16)
        qT_ref[hd * HEAD_DIM + half:(hd + 1) * HEAD_DIM, :] = (q2 * cosT + q1 * sinT).astype(BF16)

    k = jnp.dot(h, wk_ref[...], preferred_element_type=F32)
    cos2, sin2 = cos_ref[...], sin_ref[...]
    for hd in range(N_HEADS):
        kh = k[:, hd * HEAD_DIM:(hd + 1) * HEAD_DIM]
        kh = kh * _rms_scale(kh) * kmul_ref[...]
        kh = kh * cos2 + pltpu.roll(kh, half, 1) * sin2
        k_ref[:, hd * HEAD_DIM:(hd + 1) * HEAD_DIM] = kh.astype(BF16)
        for blk in range(tile // MOBA_BLOCK):
            kmean_ref[blk:blk + 1, hd * HEAD_DIM:(hd + 1) * HEAD_DIM] = jnp.mean(
                kh[blk * MOBA_BLOCK:(blk + 1) * MOBA_BLOCK, :], axis=0, keepdims=True)

    vT_ref[...] = lax.dot_general(wvT_ref[...], h, nt, preferred_element_type=F32).astype(BF16)
    gate_ref[...] = _silu(jnp.dot(h, wz_ref[...], preferred_element_type=F32)).astype(BF16)


def _rope_tables(S):
    inv = ROPE_THETA ** (-jnp.arange(0, HEAD_DIM, 2, dtype=F32) / HEAD_DIM)
    ang = jnp.arange(S, dtype=F32)[:, None] * inv[None, :]
    return jnp.cos(ang), jnp.sin(ang)


def _moba_proj(x, norm_g, w_in, q_norm, k_norm):
    B, S, _ = x.shape
    tile = SEQ_TILE
    nblk = tile // MOBA_BLOCK
    wq, wk, wv, wz = jnp.split(w_in.astype(BF16), 4, axis=1)
    cos, sin = _rope_tables(S)
    q_gain = q_norm * (HEAD_DIM ** -0.5 * math.log2(math.e))
    qmul = jnp.broadcast_to(q_gain[:, None], (HEAD_DIM, tile))
    kmul = k_norm.reshape(1, HEAD_DIM)
    tok = lambda b, t: (b, t, 0)
    feat = lambda b, t: (b, 0, t)
    return pl.pallas_call(
        functools.partial(_moba_proj_kernel, tile=tile),
        grid=(B, S // tile),
        in_specs=[
            pl.BlockSpec((None, tile, D_MODEL), tok),
            _const_spec((1, D_MODEL)),
            _const_spec((D_INNER, D_MODEL)),
            _const_spec((D_MODEL, D_INNER)),
            _const_spec((D_INNER, D_MODEL)),
            _const_spec((D_MODEL, D_INNER)),
            _const_spec((HEAD_DIM, tile)),
            _const_spec((1, HEAD_DIM)),
            pl.BlockSpec((HEAD_DIM // 2, tile), lambda b, t: (0, t)),
            pl.BlockSpec((HEAD_DIM // 2, tile), lambda b, t: (0, t)),
            pl.BlockSpec((tile, HEAD_DIM), lambda b, t: (t, 0)),
            pl.BlockSpec((tile, HEAD_DIM), lambda b, t: (t, 0)),
        ],
        out_specs=[
            pl.BlockSpec((None, D_INNER, tile), feat),
            pl.BlockSpec((None, tile, D_INNER), tok),
            pl.BlockSpec((None, D_INNER, tile), feat),
            pl.BlockSpec((None, tile, D_INNER), tok),
            pl.BlockSpec((None, None, nblk, D_INNER), lambda b, t: (b, t, 0, 0)),
        ],
        out_shape=[
            jax.ShapeDtypeStruct((B, D_INNER, S), BF16),
            jax.ShapeDtypeStruct((B, S, D_INNER), BF16),
            jax.ShapeDtypeStruct((B, D_INNER, S), BF16),
            jax.ShapeDtypeStruct((B, S, D_INNER), BF16),
            jax.ShapeDtypeStruct((B, S // tile, nblk, D_INNER), F32),
        ],
        compiler_params=pltpu.CompilerParams(
            dimension_semantics=("arbitrary", "arbitrary"), vmem_limit_bytes=VMEM_LIMIT),
        name="moba_proj",
    )(x, norm_g.reshape(1, D_MODEL), wq.T, wk, wv.T, wz, qmul, kmul,
      cos.T, sin.T, jnp.concatenate([cos, cos], axis=1), jnp.concatenate([-sin, sin], axis=1))


CHAINS_PER_STEP = 64
SCORE_LEAD = 2
OWN_LEAD = 3
TILES_PER_STEP = 8
CHOOSE_PER_STEP = 8
ONES_ROWS = 16


def _chain_tables(n_blocks):
    chains = []
    for i in range(1, n_blocks):
        js = list(range(i)) + ([i] if i % 2 else [])
        chains += [(i, js[n], js[n + 1]) for n in range(0, len(js), 2)]
    chains.sort(key=lambda c: (c[1], c[0]))
    return tuple(np.asarray([c[n] for c in chains], np.int32) for n in range(3))


def _moba_attn_kernel(qi_ref, ka_ref, kb_ref, zero_ref, qT_ref, k_ref, vT_ref, kmean_ref, gate_ref, y_ref,
                      bias_ref, m_ref, l_ref, acc_ref, *x_refs, n_blocks, n_chains):
    def block(i):
        return pl.ds(pl.multiple_of(i * MOBA_BLOCK, MOBA_BLOCK), MOBA_BLOCK)

    km = kmean_ref[...]
    km_hi = km.astype(BF16)
    km_lo = (km - km_hi.astype(F32)).astype(BF16)
    blk = lax.broadcasted_iota(jnp.int32, (n_blocks, MOBA_BLOCK), 0)

    def choose_step(g, carry):
        for u in range(CHOOSE_PER_STEP):
            i = g * CHOOSE_PER_STEP + u
            qT = qT_ref[:, block(i)]
            gate = (jnp.dot(km_hi, qT, preferred_element_type=F32)
                    + jnp.dot(km_lo, qT, preferred_element_type=F32))
            past = blk < i
            cand = jnp.where(past, gate, -jnp.inf)
            chosen = jnp.zeros((n_blocks, MOBA_BLOCK), jnp.bool_)
            for _ in range(MOBA_TOPK):
                top = jnp.max(cand, axis=0, keepdims=True)
                first = jnp.min(jnp.where(cand == top, blk, n_blocks), axis=0, keepdims=True)
                pick = blk == first
                chosen = jnp.logical_or(chosen, jnp.logical_and(pick, past))
                cand = jnp.where(pick, -jnp.inf, cand)
            bias_ref[:, block(i)] = jnp.where(chosen, 0.0, NEG_BIG).astype(F32)
        return carry

    lax.fori_loop(0, n_blocks // CHOOSE_PER_STEP, choose_step, 0)

    key_pos = lax.broadcasted_iota(jnp.int32, (MOBA_BLOCK, MOBA_BLOCK), 0)
    qry_pos = lax.broadcasted_iota(jnp.int32, (MOBA_BLOCK, MOBA_BLOCK), 1)
    causal = key_pos <= qry_pos

    def own_step(g, carry):
        s = {}
        for u in range(TILES_PER_STEP + OWN_LEAD):
            if u < TILES_PER_STEP:
                i = g * TILES_PER_STEP + u
                s[u] = jnp.dot(k_ref[block(i), :], qT_ref[:, block(i)], preferred_element_type=F32)
            if u >= OWN_LEAD:
                i = g * TILES_PER_STEP + u - OWN_LEAD
                su = jnp.where(causal, s.pop(u - OWN_LEAD), NEG_BIG)
                c = jnp.max(su, axis=0, keepdims=True)
                p = jnp.exp2(su - c)
                m_ref[i] = c
                l_ref[i] = jnp.sum(p, axis=0, keepdims=True)
                acc_ref[i] = jnp.dot(vT_ref[:, block(i)], p.astype(BF16), preferred_element_type=F32)
        return carry

    lax.fori_loop(0, n_blocks // TILES_PER_STEP, own_step, 0)

    ones = jnp.ones((ONES_ROWS, 2 * MOBA_BLOCK), BF16)

    def chain_step(g, carry):
        ids, s = {}, {}
        for u in range(CHAINS_PER_STEP + SCORE_LEAD):
            if u < CHAINS_PER_STEP:
                idx = g * CHAINS_PER_STEP + u
                ids[u] = i, ja, jb = qi_ref[idx], ka_ref[idx], kb_ref[idx]
                qT = qT_ref[:, block(i)]
                m0 = m_ref[i]
                xa = jnp.dot(k_ref[block(ja), :], qT, preferred_element_type=F32) - (
                    m0 - bias_ref[pl.ds(ja, 1), block(i)])
                xb = jnp.dot(k_ref[block(jb), :], qT, preferred_element_type=F32) - (
                    m0 - bias_ref[pl.ds(jb, 1), block(i)])
                top = jnp.maximum(jnp.max(xa, axis=0, keepdims=True), jnp.max(xb, axis=0, keepdims=True))
                x_ref = x_refs[u % len(x_refs)]
                x_ref[zero_ref[0], :MOBA_BLOCK, :] = xa.astype(BF16)
                x_ref[zero_ref[0], MOBA_BLOCK:, :] = xb.astype(BF16)
                s[u] = (m0, top)
            if u >= SCORE_LEAD:
                i, ja, jb = ids.pop(u - SCORE_LEAD)
                m0, top = s.pop(u - SCORE_LEAD)
                m_old = m_ref[i]
                shift = jnp.maximum(m_old - m0, top).astype(BF16)
                m_new = m0 + shift.astype(F32)
                p = jnp.exp2(x_refs[(u - SCORE_LEAD) % len(x_refs)][zero_ref[0]] - shift)
                v_aug = jnp.concatenate(
                    [jnp.concatenate([vT_ref[:, block(ja)], vT_ref[:, block(jb)]], axis=1), ones], axis=0)
                o_aug = jnp.dot(v_aug, p, preferred_element_type=F32)
                a_old = jnp.exp2(m_old - m_new)
                m_ref[i] = m_new
                l_ref[i] = a_old * l_ref[i] + o_aug[HEAD_DIM:HEAD_DIM + 1, :]
                acc_ref[i] = a_old * acc_ref[i] + o_aug[:HEAD_DIM, :]
        return carry

    lax.fori_loop(0, n_chains // CHAINS_PER_STEP, chain_step, 0)

    def finish_step(g, carry):
        for u in range(TILES_PER_STEP):
            i = g * TILES_PER_STEP + u
            oT = acc_ref[i] * (1.0 / l_ref[i])
            y_ref[block(i), :] = (oT.T * gate_ref[block(i), :].astype(F32)).astype(BF16)
        return carry

    lax.fori_loop(0, n_blocks // TILES_PER_STEP, finish_step, 0)


def _moba_attn(qT, k, vT, kmean, gate):
    B, S, _ = k.shape
    n_blocks = S // MOBA_BLOCK
    tables = _chain_tables(n_blocks)
    n_chains = len(tables[0])
    assert n_chains % CHAINS_PER_STEP == 0
    assert n_blocks % TILES_PER_STEP == 0 and n_blocks % CHOOSE_PER_STEP == 0
    grid_spec = pltpu.PrefetchScalarGridSpec(
        num_scalar_prefetch=4,
        grid=(B, N_HEADS),
        in_specs=[
            pl.BlockSpec((None, HEAD_DIM, S), lambda b, h, *_: (b, h, 0)),
            pl.BlockSpec((None, S, HEAD_DIM), lambda b, h, *_: (b, 0, h)),
            pl.BlockSpec((None, HEAD_DIM, S), lambda b, h, *_: (b, h, 0)),
            pl.BlockSpec((None, n_blocks, HEAD_DIM), lambda b, h, *_: (b, 0, h)),
            pl.BlockSpec((None, S, HEAD_DIM), lambda b, h, *_: (b, 0, h)),
        ],
        out_specs=pl.BlockSpec((None, S, HEAD_DIM), lambda b, h, *_: (b, 0, h)),
        scratch_shapes=[
            pltpu.VMEM((n_blocks, S), F32),
            pltpu.VMEM((n_blocks, 1, MOBA_BLOCK), F32),
            pltpu.VMEM((n_blocks, 1, MOBA_BLOCK), F32),
            pltpu.VMEM((n_blocks, HEAD_DIM, MOBA_BLOCK), F32),
        ] + [pltpu.VMEM((1, 2 * MOBA_BLOCK, MOBA_BLOCK), BF16)] * (SCORE_LEAD + 1),
    )
    return pl.pallas_call(
        functools.partial(_moba_attn_kernel, n_blocks=n_blocks, n_chains=n_chains),
        grid_spec=grid_spec,
        out_shape=jax.ShapeDtypeStruct((B, S, D_INNER), BF16),
        compiler_params=pltpu.CompilerParams(
            dimension_semantics=("arbitrary", "arbitrary"), vmem_limit_bytes=VMEM_LIMIT),
        name="moba_attn",
    )(*(jnp.asarray(t) for t in tables), jnp.zeros((1,), jnp.int32), qT, k, vT, kmean, gate)


def _out_proj_kernel(x_ref, y_ref, w_ref, o_ref):
    o_ref[...] = x_ref[...] + jnp.dot(y_ref[...], w_ref[...], preferred_element_type=F32)


def _out_proj(x, y, w_out):
    B, S, _ = x.shape
    rows = B * S
    return pl.pallas_call(
        _out_proj_kernel,
        grid=(rows // OUT_TILE,),
        in_specs=[
            pl.BlockSpec((OUT_TILE, D_MODEL), lambda t: (t, 0)),
            pl.BlockSpec((OUT_TILE, D_INNER), lambda t: (t, 0)),
            _const_spec((D_INNER, D_MODEL)),
        ],
        out_specs=pl.BlockSpec((OUT_TILE, D_MODEL), lambda t: (t, 0)),
        out_shape=jax.ShapeDtypeStruct((rows, D_MODEL), F32),
        compiler_params=pltpu.CompilerParams(
            dimension_semantics=("arbitrary",), vmem_limit_bytes=VMEM_LIMIT),
        name="moba_out_proj",
    )(x.reshape(rows, D_MODEL), y.reshape(rows, D_INNER), w_out.astype(BF16)).reshape(B, S, D_MODEL)


def kernel(x, norm0, w_in0, w_pool0, pool_scale0, w_out0, norm1, w_in1, q_norm1, k_norm1, w_out1):
    B, S, _ = x.shape
    x1 = _pool_layer(x, norm0, w_in0, w_pool0, pool_scale0, w_out0)
    qT, k, vT, gate, kmean = _moba_proj(x1, norm1, w_in1, q_norm1, k_norm1)
    y = _moba_attn(qT, k, vT, kmean.reshape(B, S // MOBA_BLOCK, D_INNER), gate)
    return _out_proj(x1, y, w_out1)
```

```python
import functools
import math

import jax
import jax.numpy as jnp
import numpy as np
from jax import lax
from jax.experimental import pallas as pl
from jax.experimental.pallas import tpu as pltpu

D_MODEL = 1024
D_INNER = 2 * D_MODEL
POOL_WINDOWS = (2, 4, 8, 16)
POOL_GROUP_DIM = D_INNER // len(POOL_WINDOWS)
HEAD_DIM = 128
N_HEADS = D_INNER // HEAD_DIM
MOBA_BLOCK = 256
MOBA_TOPK = 3
ROPE_THETA = 10000.0
EPS = 1e-6

POOL_HALO = 16
NEG_BIG = -1e30
VMEM_LIMIT = 56 * 1024 * 1024
SEQ_TILE = 512
OUT_TILE = 1024

F32 = jnp.float32
BF16 = jnp.bfloat16


def _const_spec(shape):
    nd = len(shape)
    return pl.BlockSpec(shape, lambda *_: (0,) * nd, pipeline_mode=pl.Buffered(1))


def _rms_scale(x):
    return lax.rsqrt(jnp.mean(x * x, axis=-1, keepdims=True) + EPS)


def _silu(z):
    return z / (1.0 + jnp.exp(-z))


def _pool_layer_kernel(x_ref, g_ref, win_ref, wpool_ref, pscale_ref, wout_ref, o_ref, ubuf_ref, *, tile):
    t = pl.program_id(1)
    x = x_ref[...]
    h = (x * _rms_scale(x) * g_ref[...]).astype(BF16)
    uz = jnp.dot(h, win_ref[...], preferred_element_type=F32)
    u = uz[:, :D_INNER]
    z = uz[:, D_INNER:]

    @pl.when(t == 0)
    def _():
        ubuf_ref[0:POOL_HALO, :] = jnp.zeros((POOL_HALO, D_INNER), F32)

    ubuf_ref[POOL_HALO:POOL_HALO + tile, :] = u

    pos = t * tile + lax.broadcasted_iota(jnp.int32, (tile, 1), 0) + 1
    mixed = []
    for g, w in enumerate(POOL_WINDOWS):
        lo, hi = g * POOL_GROUP_DIM, (g + 1) * POOL_GROUP_DIM
        ext = ubuf_ref[:, lo:hi]
        acc, width = ext, 1
        while width < w:
            acc = acc + pltpu.roll(acc, width, 0)
            width *= 2
        inv_cnt = 1.0 / jnp.minimum(pos, w).astype(F32)
        pooled = acc[POOL_HALO:, :] * inv_cnt - u[:, lo:hi]
        mixed.append(jnp.dot(pooled.astype(BF16), wpool_ref[g], preferred_element_type=F32))
    mixed = jnp.concatenate(mixed, axis=-1)

    ubuf_ref[0:POOL_HALO, :] = u[tile - POOL_HALO:, :]

    y = (mixed * pscale_ref[...] * _silu(z)).astype(BF16)
    o_ref[...] = x + jnp.dot(y, wout_ref[...], preferred_element_type=F32)


def _pool_layer(x, norm_g, w_in, w_pool, pool_scale, w_out):
    B, S, _ = x.shape
    tile = SEQ_TILE
    return pl.pallas_call(
        functools.partial(_pool_layer_kernel, tile=tile),
        grid=(B, S // tile),
        in_specs=[
            pl.BlockSpec((None, tile, D_MODEL), lambda b, t: (b, t, 0)),
            _const_spec((1, D_MODEL)),
            _const_spec((D_MODEL, 2 * D_INNER)),
            _const_spec((len(POOL_WINDOWS), POOL_GROUP_DIM, POOL_GROUP_DIM)),
            _const_spec((1, D_INNER)),
            _const_spec((D_INNER, D_MODEL)),
        ],
        out_specs=pl.BlockSpec((None, tile, D_MODEL), lambda b, t: (b, t, 0)),
        out_shape=jax.ShapeDtypeStruct((B, S, D_MODEL), F32),
        scratch_shapes=[pltpu.VMEM((POOL_HALO + tile, D_INNER), F32)],
        compiler_params=pltpu.CompilerParams(
            dimension_semantics=("arbitrary", "arbitrary"), vmem_limit_bytes=VMEM_LIMIT),
        name="pool_layer",
    )(x, norm_g.reshape(1, D_MODEL), w_in.astype(BF16), w_pool.astype(BF16),
      pool_scale.reshape(1, D_INNER), w_out.astype(BF16))


def _moba_proj_kernel(x_ref, g_ref, wqT_ref, wk_ref, wvT_ref, wz_ref, qmul_ref, kmul_ref,
                      cosT_ref, sinT_ref, cos_ref, sin_ref,
                      qT_ref, k_ref, vT_ref, gate_ref, kmean_ref, *, tile):
    x = x_ref[...]
    h = (x * _rms_scale(x) * g_ref[...]).astype(BF16)
    nt = (((1,), (1,)), ((), ()))

    qT = lax.dot_general(wqT_ref[...], h, nt, preferred_element_type=F32)
    cosT, sinT = cosT_ref[...], sinT_ref[...]
    half = HEAD_DIM // 2
    for hd in range(N_HEADS):
        qh = qT[hd * HEAD_DIM:(hd + 1) * HEAD_DIM, :]
        qh = qh * lax.rsqrt(jnp.mean(qh * qh, axis=0, keepdims=True) + EPS) * qmul_ref[...]
        q1, q2 = qh[:half, :], qh[half:, :]
        qT_ref[hd * HEAD_DIM:hd * HEAD_DIM + half, :] = (q1 * cosT - q2 * sinT).astype(BF16)
        qT_ref[hd * HEAD_DIM + half:(hd + 1) * HEAD_DIM, :] = (q2 * cosT + q1 * sinT).astype(BF16)

    k = jnp.dot(h, wk_ref[...], preferred_element_type=F32)
    cos2, sin2 = cos_ref[...], sin_ref[...]
    for hd in range(N_HEADS):
        kh = k[:, hd * HEAD_DIM:(hd + 1) * HEAD_DIM]
        kh = kh * _rms_scale(kh) * kmul_ref[...]
        kh = kh * cos2 + pltpu.roll(kh, half, 1) * sin2
        k_ref[:, hd * HEAD_DIM:(hd + 1) * HEAD_DIM] = kh.astype(BF16)
        for blk in range(tile // MOBA_BLOCK):
            kmean_ref[blk:blk + 1, hd * HEAD_DIM:(hd + 1) * HEAD_DIM] = jnp.mean(
                kh[blk * MOBA_BLOCK:(blk + 1) * MOBA_BLOCK, :], axis=0, keepdims=True)

    vT_ref[...] = lax.dot_general(wvT_ref[...], h, nt, preferred_element_type=F32).astype(BF16)
    gate_ref[...] = _silu(jnp.dot(h, wz_ref[...], preferred_element_type=F32)).astype(BF16)


def _rope_tables(S):
    inv = ROPE_THETA ** (-jnp.arange(0, HEAD_DIM, 2, dtype=F32) / HEAD_DIM)
    ang = jnp.arange(S, dtype=F32)[:, None] * inv[None, :]
    return jnp.cos(ang), jnp.sin(ang)


def _moba_proj(x, norm_g, w_in, q_norm, k_norm):
    B, S, _ = x.shape
    tile = SEQ_TILE
    nblk = tile // MOBA_BLOCK
    wq, wk, wv, wz = jnp.split(w_in.astype(BF16), 4, axis=1)
    cos, sin = _rope_tables(S)
    q_gain = q_norm * (HEAD_DIM ** -0.5 * math.log2(math.e))
    qmul = jnp.broadcast_to(q_gain[:, None], (HEAD_DIM, tile))
    kmul = k_norm.reshape(1, HEAD_DIM)
    tok = lambda b, t: (b, t, 0)
    feat = lambda b, t: (b, 0, t)
    return pl.pallas_call(
        functools.partial(_moba_proj_kernel, tile=tile),
        grid=(B, S // tile),
        in_specs=[
            pl.BlockSpec((None, tile, D_MODEL), tok),
            _const_spec((1, D_MODEL)),
            _const_spec((D_INNER, D_MODEL)),
            _const_spec((D_MODEL, D_INNER)),
            _const_spec((D_INNER, D_MODEL)),
            _const_spec((D_MODEL, D_INNER)),
            _const_spec((HEAD_DIM, tile)),
            _const_spec((1, HEAD_DIM)),
            pl.BlockSpec((HEAD_DIM // 2, tile), lambda b, t: (0, t)),
            pl.BlockSpec((HEAD_DIM // 2, tile), lambda b, t: (0, t)),
            pl.BlockSpec((tile, HEAD_DIM), lambda b, t: (t, 0)),
            pl.BlockSpec((tile, HEAD_DIM), lambda b, t: (t, 0)),
        ],
        out_specs=[
            pl.BlockSpec((None, D_INNER, tile), feat),
            pl.BlockSpec((None, tile, D_INNER), tok),
            pl.BlockSpec((None, D_INNER, tile), feat),
            pl.BlockSpec((None, tile, D_INNER), tok),
            pl.BlockSpec((None, None, nblk, D_INNER), lambda b, t: (b, t, 0, 0)),
        ],
        out_shape=[
            jax.ShapeDtypeStruct((B, D_INNER, S), BF16),
            jax.ShapeDtypeStruct((B, S, D_INNER), BF16),
            jax.ShapeDtypeStruct((B, D_INNER, S), BF16),
            jax.ShapeDtypeStruct((B, S, D_INNER), BF16),
            jax.ShapeDtypeStruct((B, S // tile, nblk, D_INNER), F32),
        ],
        compiler_params=pltpu.CompilerParams(
            dimension_semantics=("arbitrary", "arbitrary"), vmem_limit_bytes=VMEM_LIMIT),
        name="moba_proj",
    )(x, norm_g.reshape(1, D_MODEL), wq.T, wk, wv.T, wz, qmul, kmul,
      cos.T, sin.T, jnp.concatenate([cos, cos], axis=1), jnp.concatenate([-sin, sin], axis=1))


CHAINS_PER_STEP = 64
SCORE_LEAD = 2
OWN_LEAD = 3
TILES_PER_STEP = 8
ONES_ROWS = 16


def _chain_tables(n_blocks):
    chains = []
    for i in range(1, n_blocks):
        js = list(range(i)) + ([i] if i % 2 else [])
        chains += [(i, js[n], js[n + 1]) for n in range(0, len(js), 2)]
    chains.sort(key=lambda c: (c[1], c[0]))
    return tuple(np.asarray([c[n] for c in chains], np.int32) for n in range(3))


def _moba_attn_kernel(qi_ref, ka_ref, kb_ref, zero_ref, qT_ref, k_ref, vT_ref, kmean_ref, gate_ref, y_ref,
                      bias_ref, m_ref, l_ref, acc_ref, *x_refs, n_blocks, n_chains):
    def block(i):
        return pl.ds(pl.multiple_of(i * MOBA_BLOCK, MOBA_BLOCK), MOBA_BLOCK)

    km = kmean_ref[...]
    km_hi = km.astype(BF16)
    km_lo = (km - km_hi.astype(F32)).astype(BF16)
    blk = lax.broadcasted_iota(jnp.int32, (n_blocks, MOBA_BLOCK), 0)
    key_pos = lax.broadcasted_iota(jnp.int32, (MOBA_BLOCK, MOBA_BLOCK), 0)
    qry_pos = lax.broadcasted_iota(jnp.int32, (MOBA_BLOCK, MOBA_BLOCK), 1)
    causal = key_pos <= qry_pos
    ones = jnp.ones((ONES_ROWS, 2 * MOBA_BLOCK), BF16)

    def choose(i, qT):
        gate = (jnp.dot(km_hi, qT, preferred_element_type=F32)
                + jnp.dot(km_lo, qT, preferred_element_type=F32))
        past = blk < i
        cand = jnp.where(past, gate, -jnp.inf)
        chosen = jnp.zeros((n_blocks, MOBA_BLOCK), jnp.bool_)
        for _ in range(MOBA_TOPK):
            top = jnp.max(cand, axis=0, keepdims=True)
            first = jnp.min(jnp.where(cand == top, blk, n_blocks), axis=0, keepdims=True)
            pick = blk == first
            chosen = jnp.logical_or(chosen, jnp.logical_and(pick, past))
            cand = jnp.where(pick, -jnp.inf, cand)
        bias_ref[:, block(i)] = jnp.where(chosen, 0.0, NEG_BIG).astype(F32)

    def own_step(g, carry):
        s = {}
        for u in range(TILES_PER_STEP + OWN_LEAD):
            if u < TILES_PER_STEP:
                i = g * TILES_PER_STEP + u
                qT = qT_ref[:, block(i)]
                s[u] = jnp.dot(k_ref[block(i), :], qT, preferred_element_type=F32)
                choose(i, qT)
            if u >= OWN_LEAD:
                i = g * TILES_PER_STEP + u - OWN_LEAD
                su = jnp.where(causal, s.pop(u - OWN_LEAD), NEG_BIG)
                c = jnp.max(su, axis=0, keepdims=True)
                p = jnp.exp2(su - c).astype(BF16)
                v_aug = jnp.concatenate([vT_ref[:, block(i)], ones[:, :MOBA_BLOCK]], axis=0)
                o_aug = jnp.dot(v_aug, p, preferred_element_type=F32)
                m_ref[i] = c
                l_ref[i] = o_aug[HEAD_DIM:HEAD_DIM + 1, :]
                acc_ref[i] = o_aug[:HEAD_DIM, :]
        return carry

    lax.fori_loop(0, n_blocks // TILES_PER_STEP, own_step, 0)


    def chain_step(g, carry):
        ids, s = {}, {}
        for u in range(CHAINS_PER_STEP + SCORE_LEAD):
            if u < CHAINS_PER_STEP:
                idx = g * CHAINS_PER_STEP + u
                ids[u] = i, ja, jb = qi_ref[idx], ka_ref[idx], kb_ref[idx]
                qT = qT_ref[:, block(i)]
                m0 = m_ref[i]
                xa = jnp.dot(k_ref[block(ja), :], qT, preferred_element_type=F32) - (
                    m0 - bias_ref[pl.ds(ja, 1), block(i)])
                xb = jnp.dot(k_ref[block(jb), :], qT, preferred_element_type=F32) - (
                    m0 - bias_ref[pl.ds(jb, 1), block(i)])
                top = jnp.maximum(jnp.max(xa, axis=0, keepdims=True), jnp.max(xb, axis=0, keepdims=True))
                x_ref = x_refs[u % len(x_refs)]
                x_ref[zero_ref[0], :MOBA_BLOCK, :] = xa.astype(BF16)
                x_ref[zero_ref[0], MOBA_BLOCK:, :] = xb.astype(BF16)
                s[u] = (m0, top)
            if u >= SCORE_LEAD:
                i, ja, jb = ids.pop(u - SCORE_LEAD)
                m0, top = s.pop(u - SCORE_LEAD)
                m_old = m_ref[i]
                shift = jnp.maximum(m_old - m0, top).astype(BF16)
                m_new = m0 + shift.astype(F32)
                p = jnp.exp2(x_refs[(u - SCORE_LEAD) % len(x_refs)][zero_ref[0]] - shift)
                v_aug = jnp.concatenate(
                    [jnp.concatenate([vT_ref[:, block(ja)], vT_ref[:, block(jb)]], axis=1), ones], axis=0)
                o_aug = jnp.dot(v_aug, p, preferred_element_type=F32)
                a_old = jnp.exp2(m_old - m_new)
                m_ref[i] = m_new
                l_ref[i] = a_old * l_ref[i] + o_aug[HEAD_DIM:HEAD_DIM + 1, :]
                acc_ref[i] = a_old * acc_ref[i] + o_aug[:HEAD_DIM, :]
        return carry

    lax.fori_loop(0, n_chains // CHAINS_PER_STEP, chain_step, 0)

    def finish_step(g, carry):
        for u in range(TILES_PER_STEP):
            i = g * TILES_PER_STEP + u
            oT = acc_ref[i] * (1.0 / l_ref[i])
            y_ref[block(i), :] = (oT.T * gate_ref[block(i), :].astype(F32)).astype(BF16)
        return carry

    lax.fori_loop(0, n_blocks // TILES_PER_STEP, finish_step, 0)


def _moba_attn(qT, k, vT, kmean, gate):
    B, S, _ = k.shape
    n_blocks = S // MOBA_BLOCK
    tables = _chain_tables(n_blocks)
    n_chains = len(tables[0])
    assert n_chains % CHAINS_PER_STEP == 0
    assert n_blocks % TILES_PER_STEP == 0
    grid_spec = pltpu.PrefetchScalarGridSpec(
        num_scalar_prefetch=4,
        grid=(B, N_HEADS),
        in_specs=[
            pl.BlockSpec((None, HEAD_DIM, S), lambda b, h, *_: (b, h, 0)),
            pl.BlockSpec((None, S, HEAD_DIM), lambda b, h, *_: (b, 0, h)),
            pl.BlockSpec((None, HEAD_DIM, S), lambda b, h, *_: (b, h, 0)),
            pl.BlockSpec((None, n_blocks, HEAD_DIM), lambda b, h, *_: (b, 0, h)),
            pl.BlockSpec((None, S, HEAD_DIM), lambda b, h, *_: (b, 0, h)),
        ],
        out_specs=pl.BlockSpec((None, S, HEAD_DIM), lambda b, h, *_: (b, 0, h)),
        scratch_shapes=[
            pltpu.VMEM((n_blocks, S), F32),
            pltpu.VMEM((n_blocks, 1, MOBA_BLOCK), F32),
            pltpu.VMEM((n_blocks, 1, MOBA_BLOCK), F32),
            pltpu.VMEM((n_blocks, HEAD_DIM, MOBA_BLOCK), F32),
        ] + [pltpu.VMEM((1, 2 * MOBA_BLOCK, MOBA_BLOCK), BF16)] * (SCORE_LEAD + 1),
    )
    return pl.pallas_call(
        functools.partial(_moba_attn_kernel, n_blocks=n_blocks, n_chains=n_chains),
        grid_spec=grid_spec,
        out_shape=jax.ShapeDtypeStruct((B, S, D_INNER), BF16),
        compiler_params=pltpu.CompilerParams(
            dimension_semantics=("arbitrary", "arbitrary"), vmem_limit_bytes=VMEM_LIMIT),
        name="moba_attn",
    )(*(jnp.asarray(t) for t in tables), jnp.zeros((1,), jnp.int32), qT, k, vT, kmean, gate)


def _out_proj_kernel(x_ref, y_ref, w_ref, o_ref):
    o_ref[...] = x_ref[...] + jnp.dot(y_ref[...], w_ref[...], preferred_element_type=F32)


def _out_proj(x, y, w_out):
    B, S, _ = x.shape
    rows = B * S
    return pl.pallas_call(
        _out_proj_kernel,
        grid=(rows // OUT_TILE,),
        in_specs=[
            pl.BlockSpec((OUT_TILE, D_MODEL), lambda t: (t, 0)),
            pl.BlockSpec((OUT_TILE, D_INNER), lambda t: (t, 0)),
            _const_spec((D_INNER, D_MODEL)),
        ],
        out_specs=pl.BlockSpec((OUT_TILE, D_MODEL), lambda t: (t, 0)),
        out_shape=jax.ShapeDtypeStruct((rows, D_MODEL), F32),
        compiler_params=pltpu.CompilerParams(
            dimension_semantics=("arbitrary",), vmem_limit_bytes=VMEM_LIMIT),
        name="moba_out_proj",
    )(x.reshape(rows, D_MODEL), y.reshape(rows, D_INNER), w_out.astype(BF16)).reshape(B, S, D_MODEL)


def kernel(x, norm0, w_in0, w_pool0, pool_scale0, w_out0, norm1, w_in1, q_norm1, k_norm1, w_out1):
    B, S, _ = x.shape
    x1 = _pool_layer(x, norm0, w_in0, w_pool0, pool_scale0, w_out0)
    qT, k, vT, gate, kmean = _moba_proj(x1, norm1, w_in1, q_norm1, k_norm1)
    y = _moba_attn(qT, k, vT, kmean.reshape(B, S // MOBA_BLOCK, D_INNER), gate)
    return _out_proj(x1, y, w_out1)
```

```python
import functools
import math

import jax
import jax.numpy as jnp
import numpy as np
from jax import lax
from jax.experimental import pallas as pl
from jax.experimental.pallas import tpu as pltpu

D_MODEL = 1024
D_INNER = 2 * D_MODEL
POOL_WINDOWS = (2, 4, 8, 16)
POOL_GROUP_DIM = D_INNER // len(POOL_WINDOWS)
HEAD_DIM = 128
N_HEADS = D_INNER // HEAD_DIM
MOBA_BLOCK = 256
MOBA_TOPK = 3
ROPE_THETA = 10000.0
EPS = 1e-6

POOL_HALO = 16
NEG_BIG = -1e30
VMEM_LIMIT = 56 * 1024 * 1024
SEQ_TILE = 512
OUT_TILE = 1024
PROJ_GROUP = 256

F32 = jnp.float32
BF16 = jnp.bfloat16


def _const_spec(shape):
    nd = len(shape)
    return pl.BlockSpec(shape, lambda *_: (0,) * nd, pipeline_mode=pl.Buffered(1))


def _rms_scale(x):
    return lax.rsqrt(jnp.mean(x * x, axis=-1, keepdims=True) + EPS)


def _silu(z):
    return z / (1.0 + jnp.exp(-z))


def _pool_layer_kernel(x_ref, g_ref, win_ref, wpool_ref, pscale_ref, wout_ref, o_ref, ubuf_ref, *, tile):
    t = pl.program_id(1)
    x = x_ref[...]
    h = (x * _rms_scale(x) * g_ref[...]).astype(BF16)

    @pl.when(t == 0)
    def _():
        ubuf_ref[0:POOL_HALO, :] = jnp.zeros((POOL_HALO, D_INNER), F32)

    pos = t * tile + lax.broadcasted_iota(jnp.int32, (tile, 1), 0) + 1
    y = []
    for g, w in enumerate(POOL_WINDOWS):
        lo, hi = g * POOL_GROUP_DIM, (g + 1) * POOL_GROUP_DIM
        u = jnp.dot(h, win_ref[:, lo:hi], preferred_element_type=F32)
        ubuf_ref[POOL_HALO:POOL_HALO + tile, lo:hi] = u
        ext = ubuf_ref[:, lo:hi]
        acc, width = ext, 1
        while width < w:
            acc = acc + pltpu.roll(acc, width, 0)
            width *= 2
        inv_cnt = 1.0 / jnp.minimum(pos, w).astype(F32)
        pooled = acc[POOL_HALO:, :] * inv_cnt - u
        mixed = jnp.dot(pooled.astype(BF16), wpool_ref[g], preferred_element_type=F32)
        ubuf_ref[0:POOL_HALO, lo:hi] = u[tile - POOL_HALO:, :]
        z = jnp.dot(h, win_ref[:, D_INNER + lo:D_INNER + hi], preferred_element_type=F32)
        y.append((mixed * pscale_ref[:, lo:hi] * _silu(z)).astype(BF16))

    y = jnp.concatenate(y, axis=-1)
    o_ref[...] = x + jnp.dot(y, wout_ref[...], preferred_element_type=F32)


def _pool_layer(x, norm_g, w_in, w_pool, pool_scale, w_out):
    B, S, _ = x.shape
    tile = SEQ_TILE
    return pl.pallas_call(
        functools.partial(_pool_layer_kernel, tile=tile),
        grid=(B, S // tile),
        in_specs=[
            pl.BlockSpec((None, tile, D_MODEL), lambda b, t: (b, t, 0)),
            _const_spec((1, D_MODEL)),
            _const_spec((D_MODEL, 2 * D_INNER)),
            _const_spec((len(POOL_WINDOWS), POOL_GROUP_DIM, POOL_GROUP_DIM)),
            _const_spec((1, D_INNER)),
            _const_spec((D_INNER, D_MODEL)),
        ],
        out_specs=pl.BlockSpec((None, tile, D_MODEL), lambda b, t: (b, t, 0)),
        out_shape=jax.ShapeDtypeStruct((B, S, D_MODEL), F32),
        scratch_shapes=[pltpu.VMEM((POOL_HALO + tile, D_INNER), F32)],
        compiler_params=pltpu.CompilerParams(
            dimension_semantics=("arbitrary", "arbitrary"), vmem_limit_bytes=VMEM_LIMIT),
        name="pool_layer",
    )(x, norm_g.reshape(1, D_MODEL), w_in.astype(BF16), w_pool.astype(BF16),
      pool_scale.reshape(1, D_INNER), w_out.astype(BF16))


def _moba_proj_kernel(x_ref, g_ref, wqT_ref, wk_ref, wvT_ref, wz_ref, qmul_ref, kmul_ref,
                      cosT_ref, sinT_ref, cos_ref, sin_ref,
                      qT_ref, k_ref, vT_ref, gate_ref, kmean_ref, *, tile):
    x = x_ref[...]
    h = (x * _rms_scale(x) * g_ref[...]).astype(BF16)
    nt = (((1,), (1,)), ((), ()))

    cosT, sinT = cosT_ref[...], sinT_ref[...]
    cos2, sin2 = cos_ref[...], sin_ref[...]
    half = HEAD_DIM // 2
    for grp in range(D_INNER // PROJ_GROUP):
        lo = grp * PROJ_GROUP
        qT = lax.dot_general(wqT_ref[lo:lo + PROJ_GROUP, :], h, nt, preferred_element_type=F32)
        for hd in range(PROJ_GROUP // HEAD_DIM):
            r0 = lo + hd * HEAD_DIM
            qh = qT[hd * HEAD_DIM:(hd + 1) * HEAD_DIM, :]
            qh = qh * lax.rsqrt(jnp.mean(qh * qh, axis=0, keepdims=True) + EPS) * qmul_ref[...]
            q1, q2 = qh[:half, :], qh[half:, :]
            qT_ref[r0:r0 + half, :] = (q1 * cosT - q2 * sinT).astype(BF16)
            qT_ref[r0 + half:r0 + HEAD_DIM, :] = (q2 * cosT + q1 * sinT).astype(BF16)

        k = jnp.dot(h, wk_ref[:, lo:lo + PROJ_GROUP], preferred_element_type=F32)
        for hd in range(PROJ_GROUP // HEAD_DIM):
            r0 = lo + hd * HEAD_DIM
            kh = k[:, hd * HEAD_DIM:(hd + 1) * HEAD_DIM]
            kh = kh * _rms_scale(kh) * kmul_ref[...]
            kh = kh * cos2 + pltpu.roll(kh, half, 1) * sin2
            k_ref[:, r0:r0 + HEAD_DIM] = kh.astype(BF16)
            for blk in range(tile // MOBA_BLOCK):
                kmean_ref[blk:blk + 1, r0:r0 + HEAD_DIM] = jnp.mean(
                    kh[blk * MOBA_BLOCK:(blk + 1) * MOBA_BLOCK, :], axis=0, keepdims=True)

        vT_ref[lo:lo + PROJ_GROUP, :] = lax.dot_general(
            wvT_ref[lo:lo + PROJ_GROUP, :], h, nt, preferred_element_type=F32).astype(BF16)
        gate_ref[:, lo:lo + PROJ_GROUP] = _silu(
            jnp.dot(h, wz_ref[:, lo:lo + PROJ_GROUP], preferred_element_type=F32)).astype(BF16)


def _rope_tables(S):
    inv = ROPE_THETA ** (-jnp.arange(0, HEAD_DIM, 2, dtype=F32) / HEAD_DIM)
    ang = jnp.arange(S, dtype=F32)[:, None] * inv[None, :]
    return jnp.cos(ang), jnp.sin(ang)


def _moba_proj(x, norm_g, w_in, q_norm, k_norm):
    B, S, _ = x.shape
    tile = SEQ_TILE
    nblk = tile // MOBA_BLOCK
    wq, wk, wv, wz = jnp.split(w_in.astype(BF16), 4, axis=1)
    cos, sin = _rope_tables(S)
    q_gain = q_norm * (HEAD_DIM ** -0.5 * math.log2(math.e))
    qmul = jnp.broadcast_to(q_gain[:, None], (HEAD_DIM, tile))
    kmul = k_norm.reshape(1, HEAD_DIM)
    tok = lambda b, t: (b, t, 0)
    feat = lambda b, t: (b, 0, t)
    return pl.pallas_call(
        functools.partial(_moba_proj_kernel, tile=tile),
        grid=(B, S // tile),
        in_specs=[
            pl.BlockSpec((None, tile, D_MODEL), tok),
            _const_spec((1, D_MODEL)),
            _const_spec((D_INNER, D_MODEL)),
            _const_spec((D_MODEL, D_INNER)),
            _const_spec((D_INNER, D_MODEL)),
            _const_spec((D_MODEL, D_INNER)),
            _const_spec((HEAD_DIM, tile)),
            _const_spec((1, HEAD_DIM)),
            pl.BlockSpec((HEAD_DIM // 2, tile), lambda b, t: (0, t)),
            pl.BlockSpec((HEAD_DIM // 2, tile), lambda b, t: (0, t)),
            pl.BlockSpec((tile, HEAD_DIM), lambda b, t: (t, 0)),
            pl.BlockSpec((tile, HEAD_DIM), lambda b, t: (t, 0)),
        ],
        out_specs=[
            pl.BlockSpec((None, D_INNER, tile), feat),
            pl.BlockSpec((None, tile, D_INNER), tok),
            pl.BlockSpec((None, D_INNER, tile), feat),
            pl.BlockSpec((None, tile, D_INNER), tok),
            pl.BlockSpec((None, None, nblk, D_INNER), lambda b, t: (b, t, 0, 0)),
        ],
        out_shape=[
            jax.ShapeDtypeStruct((B, D_INNER, S), BF16),
            jax.ShapeDtypeStruct((B, S, D_INNER), BF16),
            jax.ShapeDtypeStruct((B, D_INNER, S), BF16),
            jax.ShapeDtypeStruct((B, S, D_INNER), BF16),
            jax.ShapeDtypeStruct((B, S // tile, nblk, D_INNER), F32),
        ],
        compiler_params=pltpu.CompilerParams(
            dimension_semantics=("arbitrary", "arbitrary"), vmem_limit_bytes=VMEM_LIMIT),
        name="moba_proj",
    )(x, norm_g.reshape(1, D_MODEL), wq.T, wk, wv.T, wz, qmul, kmul,
      cos.T, sin.T, jnp.concatenate([cos, cos], axis=1), jnp.concatenate([-sin, sin], axis=1))


CHAINS_PER_STEP = 64
SCORE_LEAD = 2
OWN_LEAD = 3
TILES_PER_STEP = 8
ONES_ROWS = 16


def _chain_tables(n_blocks):
    chains = []
    for i in range(1, n_blocks):
        js = list(range(i)) + ([i] if i % 2 else [])
        chains += [(i, js[n], js[n + 1]) for n in range(0, len(js), 2)]
    chains.sort(key=lambda c: (c[1], c[0]))
    return tuple(np.asarray([c[n] for c in chains], np.int32) for n in range(3))


def _moba_attn_kernel(qi_ref, ka_ref, kb_ref, zero_ref, qT_ref, k_ref, vT_ref, kmean_ref, gate_ref, y_ref,
                      bias_ref, m_ref, l_ref, acc_ref, *x_refs, n_blocks, n_chains):
    def block(i):
        return pl.ds(pl.multiple_of(i * MOBA_BLOCK, MOBA_BLOCK), MOBA_BLOCK)

    km = kmean_ref[...]
    km_hi = km.astype(BF16)
    km_lo = (km - km_hi.astype(F32)).astype(BF16)
    blk = lax.broadcasted_iota(jnp.int32, (n_blocks, MOBA_BLOCK), 0)
    key_pos = lax.broadcasted_iota(jnp.int32, (MOBA_BLOCK, MOBA_BLOCK), 0)
    qry_pos = lax.broadcasted_iota(jnp.int32, (MOBA_BLOCK, MOBA_BLOCK), 1)
    causal = key_pos <= qry_pos
    ones = jnp.ones((ONES_ROWS, 2 * MOBA_BLOCK), BF16)

    def choose(i, qT):
        gate = (jnp.dot(km_hi, qT, preferred_element_type=F32)
                + jnp.dot(km_lo, qT, preferred_element_type=F32))
        past = blk < i
        cand = jnp.where(past, gate, -jnp.inf)
        chosen = jnp.zeros((n_blocks, MOBA_BLOCK), jnp.bool_)
        for _ in range(MOBA_TOPK):
            top = jnp.max(cand, axis=0, keepdims=True)
            first = jnp.min(jnp.where(cand == top, blk, n_blocks), axis=0, keepdims=True)
            pick = blk == first
            chosen = jnp.logical_or(chosen, jnp.logical_and(pick, past))
            cand = jnp.where(pick, -jnp.inf, cand)
        bias_ref[:, block(i)] = jnp.where(chosen, 0.0, NEG_BIG).astype(F32)

    def own_step(g, carry):
        s = {}
        for u in range(TILES_PER_STEP + OWN_LEAD):
            if u < TILES_PER_STEP:
                i = g * TILES_PER_STEP + u
                qT = qT_ref[:, block(i)]
                s[u] = jnp.dot(k_ref[block(i), :], qT, preferred_element_type=F32)
                choose(i, qT)
            if u >= OWN_LEAD:
                i = g * TILES_PER_STEP + u - OWN_LEAD
                su = jnp.where(causal, s.pop(u - OWN_LEAD), NEG_BIG)
                c = jnp.max(su, axis=0, keepdims=True)
                p = jnp.exp2(su - c).astype(BF16)
                v_aug = jnp.concatenate([vT_ref[:, block(i)], ones[:, :MOBA_BLOCK]], axis=0)
                o_aug = jnp.dot(v_aug, p, preferred_element_type=F32)
                m_ref[i] = c
                l_ref[i] = o_aug[HEAD_DIM:HEAD_DIM + 1, :]
                acc_ref[i] = o_aug[:HEAD_DIM, :]
        return carry

    lax.fori_loop(0, n_blocks // TILES_PER_STEP, own_step, 0)


    def chain_step(g, carry):
        ids, s = {}, {}
        for u in range(CHAINS_PER_STEP + SCORE_LEAD):
            if u < CHAINS_PER_STEP:
                idx = g * CHAINS_PER_STEP + u
                ids[u] = i, ja, jb = qi_ref[idx], ka_ref[idx], kb_ref[idx]
                qT = qT_ref[:, block(i)]
                m0 = m_ref[i]
                xa = jnp.dot(k_ref[block(ja), :], qT, preferred_element_type=F32) - (
                    m0 - bias_ref[pl.ds(ja, 1), block(i)])
                xb = jnp.dot(k_ref[block(jb), :], qT, preferred_element_type=F32) - (
                    m0 - bias_ref[pl.ds(jb, 1), block(i)])
                top = jnp.maximum(jnp.max(xa, axis=0, keepdims=True), jnp.max(xb, axis=0, keepdims=True))
                x_ref = x_refs[u % len(x_refs)]
                x_ref[zero_ref[0], :MOBA_BLOCK, :] = xa.astype(BF16)
                x_ref[zero_ref[0], MOBA_BLOCK:, :] = xb.astype(BF16)
                s[u] = (m0, top)
            if u >= SCORE_LEAD:
                i, ja, jb = ids.pop(u - SCORE_LEAD)
                m0, top = s.pop(u - SCORE_LEAD)
                m_old = m_ref[i]
                shift = jnp.maximum(m_old - m0, top).astype(BF16)
                m_new = m0 + shift.astype(F32)
                p = jnp.exp2(x_refs[(u - SCORE_LEAD) % len(x_refs)][zero_ref[0]] - shift)
                v_aug = jnp.concatenate(
                    [jnp.concatenate([vT_ref[:, block(ja)], vT_ref[:, block(jb)]], axis=1), ones], axis=0)
                o_aug = jnp.dot(v_aug, p, preferred_element_type=F32)
                a_old = jnp.exp2(m_old - m_new)
                m_ref[i] = m_new
                l_ref[i] = a_old * l_ref[i] + o_aug[HEAD_DIM:HEAD_DIM + 1, :]
                acc_ref[i] = a_old * acc_ref[i] + o_aug[:HEAD_DIM, :]
        return carry

    lax.fori_loop(0, n_chains // CHAINS_PER_STEP, chain_step, 0)

    def finish_step(g, carry):
        for u in range(TILES_PER_STEP):
            i = g * TILES_PER_STEP + u
            oT = acc_ref[i] * (1.0 / l_ref[i])
            y_ref[block(i), :] = (oT.T * gate_ref[block(i), :].astype(F32)).astype(BF16)
        return carry

    lax.fori_loop(0, n_blocks // TILES_PER_STEP, finish_step, 0)


def _moba_attn(qT, k, vT, kmean, gate):
    B, S, _ = k.shape
    n_blocks = S // MOBA_BLOCK
    tables = _chain_tables(n_blocks)
    n_chains = len(tables[0])
    assert n_chains % CHAINS_PER_STEP == 0
    assert n_blocks % TILES_PER_STEP == 0
    grid_spec = pltpu.PrefetchScalarGridSpec(
        num_scalar_prefetch=4,
        grid=(B, N_HEADS),
        in_specs=[
            pl.BlockSpec((None, HEAD_DIM, S), lambda b, h, *_: (b, h, 0)),
            pl.BlockSpec((None, S, HEAD_DIM), lambda b, h, *_: (b, 0, h)),
            pl.BlockSpec((None, HEAD_DIM, S), lambda b, h, *_: (b, h, 0)),
            pl.BlockSpec((None, n_blocks, HEAD_DIM), lambda b, h, *_: (b, 0, h)),
            pl.BlockSpec((None, S, HEAD_DIM), lambda b, h, *_: (b, 0, h)),
        ],
        out_specs=pl.BlockSpec((None, S, HEAD_DIM), lambda b, h, *_: (b, 0, h)),
        scratch_shapes=[
            pltpu.VMEM((n_blocks, S), F32),
            pltpu.VMEM((n_blocks, 1, MOBA_BLOCK), F32),
            pltpu.VMEM((n_blocks, 1, MOBA_BLOCK), F32),
            pltpu.VMEM((n_blocks, HEAD_DIM, MOBA_BLOCK), F32),
        ] + [pltpu.VMEM((1, 2 * MOBA_BLOCK, MOBA_BLOCK), BF16)] * (SCORE_LEAD + 1),
    )
    return pl.pallas_call(
        functools.partial(_moba_attn_kernel, n_blocks=n_blocks, n_chains=n_chains),
        grid_spec=grid_spec,
        out_shape=jax.ShapeDtypeStruct((B, S, D_INNER), BF16),
        compiler_params=pltpu.CompilerParams(
            dimension_semantics=("arbitrary", "arbitrary"), vmem_limit_bytes=VMEM_LIMIT),
        name="moba_attn",
    )(*(jnp.asarray(t) for t in tables), jnp.zeros((1,), jnp.int32), qT, k, vT, kmean, gate)


def _out_proj_kernel(x_ref, y_ref, w_ref, o_ref):
    o_ref[...] = x_ref[...] + jnp.dot(y_ref[...], w_ref[...], preferred_element_type=F32)


def _out_proj(x, y, w_out):
    B, S, _ = x.shape
    rows = B * S
    return pl.pallas_call(
        _out_proj_kernel,
        grid=(rows // OUT_TILE,),
        in_specs=[
            pl.BlockSpec((OUT_TILE, D_MODEL), lambda t: (t, 0)),
            pl.BlockSpec((OUT_TILE, D_INNER), lambda t: (t, 0)),
            _const_spec((D_INNER, D_MODEL)),
        ],
        out_specs=pl.BlockSpec((OUT_TILE, D_MODEL), lambda t: (t, 0)),
        out_shape=jax.ShapeDtypeStruct((rows, D_MODEL), F32),
        compiler_params=pltpu.CompilerParams(
            dimension_semantics=("arbitrary",), vmem_limit_bytes=VMEM_LIMIT),
        name="moba_out_proj",
    )(x.reshape(rows, D_MODEL), y.reshape(rows, D_INNER), w_out.astype(BF16)).reshape(B, S, D_MODEL)


def kernel(x, norm0, w_in0, w_pool0, pool_scale0, w_out0, norm1, w_in1, q_norm1, k_norm1, w_out1):
    B, S, _ = x.shape
    x1 = _pool_layer(x, norm0, w_in0, w_pool0, pool_scale0, w_out0)
    qT, k, vT, gate, kmean = _moba_proj(x1, norm1, w_in1, q_norm1, k_norm1)
    y = _moba_attn(qT, k, vT, kmean.reshape(B, S // MOBA_BLOCK, D_INNER), gate)
    return _out_proj(x1, y, w_out1)
```

```python
import functools
import math

import jax
import jax.numpy as jnp
import numpy as np
from jax import lax
from jax.experimental import pallas as pl
from jax.experimental.pallas import tpu as pltpu

D_MODEL = 1024
D_INNER = 2 * D_MODEL
POOL_WINDOWS = (2, 4, 8, 16)
POOL_GROUP_DIM = D_INNER // len(POOL_WINDOWS)
HEAD_DIM = 128
N_HEADS = D_INNER // HEAD_DIM
MOBA_BLOCK = 256
MOBA_TOPK = 3
ROPE_THETA = 10000.0
EPS = 1e-6

POOL_HALO = 16
NEG_BIG = -1e30
VMEM_LIMIT = 56 * 1024 * 1024
SEQ_TILE = 512
POOL_TILE = 1024
OUT_TILE = 1024
PROJ_GROUP = 256

F32 = jnp.float32
BF16 = jnp.bfloat16


def _const_spec(shape):
    nd = len(shape)
    return pl.BlockSpec(shape, lambda *_: (0,) * nd, pipeline_mode=pl.Buffered(1))


def _rms_scale(x):
    return lax.rsqrt(jnp.mean(x * x, axis=-1, keepdims=True) + EPS)


def _silu(z):
    return z / (1.0 + jnp.exp(-z))


def _pool_layer_kernel(x_ref, g_ref, win_ref, wpool_ref, pscale_ref, wout_ref, o_ref, ubuf_ref, *, tile):
    t = pl.program_id(1)
    x = x_ref[...]
    h = (x * _rms_scale(x) * g_ref[...]).astype(BF16)

    @pl.when(t == 0)
    def _():
        ubuf_ref[0:POOL_HALO, :] = jnp.zeros((POOL_HALO, D_INNER), F32)

    pos = t * tile + lax.broadcasted_iota(jnp.int32, (tile, 1), 0) + 1
    y = []
    for g, w in enumerate(POOL_WINDOWS):
        lo, hi = g * POOL_GROUP_DIM, (g + 1) * POOL_GROUP_DIM
        u = jnp.dot(h, win_ref[:, lo:hi], preferred_element_type=F32)
        ubuf_ref[POOL_HALO:POOL_HALO + tile, lo:hi] = u
        ext = ubuf_ref[:, lo:hi]
        acc, width = ext, 1
        while width < w:
            acc = acc + pltpu.roll(acc, width, 0)
            width *= 2
        inv_cnt = 1.0 / jnp.minimum(pos, w).astype(F32)
        pooled = acc[POOL_HALO:, :] * inv_cnt - u
        mixed = jnp.dot(pooled.astype(BF16), wpool_ref[g], preferred_element_type=F32)
        ubuf_ref[0:POOL_HALO, lo:hi] = u[tile - POOL_HALO:, :]
        z = jnp.dot(h, win_ref[:, D_INNER + lo:D_INNER + hi], preferred_element_type=F32)
        y.append((mixed * pscale_ref[:, lo:hi] * _silu(z)).astype(BF16))

    y = jnp.concatenate(y, axis=-1)
    o_ref[...] = x + jnp.dot(y, wout_ref[...], preferred_element_type=F32)


def _pool_layer(x, norm_g, w_in, w_pool, pool_scale, w_out):
    B, S, _ = x.shape
    tile = POOL_TILE
    return pl.pallas_call(
        functools.partial(_pool_layer_kernel, tile=tile),
        grid=(B, S // tile),
        in_specs=[
            pl.BlockSpec((None, tile, D_MODEL), lambda b, t: (b, t, 0)),
            _const_spec((1, D_MODEL)),
            _const_spec((D_MODEL, 2 * D_INNER)),
            _const_spec((len(POOL_WINDOWS), POOL_GROUP_DIM, POOL_GROUP_DIM)),
            _const_spec((1, D_INNER)),
            _const_spec((D_INNER, D_MODEL)),
        ],
        out_specs=pl.BlockSpec((None, tile, D_MODEL), lambda b, t: (b, t, 0)),
        out_shape=jax.ShapeDtypeStruct((B, S, D_MODEL), F32),
        scratch_shapes=[pltpu.VMEM((POOL_HALO + tile, D_INNER), F32)],
        compiler_params=pltpu.CompilerParams(
            dimension_semantics=("arbitrary", "arbitrary"), vmem_limit_bytes=VMEM_LIMIT),
        name="pool_layer",
    )(x, norm_g.reshape(1, D_MODEL), w_in.astype(BF16), w_pool.astype(BF16),
      pool_scale.reshape(1, D_INNER), w_out.astype(BF16))


def _moba_proj_kernel(x_ref, g_ref, wqT_ref, wk_ref, wvT_ref, wz_ref, qmul_ref, kmul_ref,
                      cosT_ref, sinT_ref, cos_ref, sin_ref,
                      qT_ref, k_ref, vT_ref, gate_ref, kmean_ref, *, tile):
    x = x_ref[...]
    h = (x * _rms_scale(x) * g_ref[...]).astype(BF16)
    nt = (((1,), (1,)), ((), ()))

    cosT, sinT = cosT_ref[...], sinT_ref[...]
    cos2, sin2 = cos_ref[...], sin_ref[...]
    half = HEAD_DIM // 2
    for grp in range(D_INNER // PROJ_GROUP):
        lo = grp * PROJ_GROUP
        qT = lax.dot_general(wqT_ref[lo:lo + PROJ_GROUP, :], h, nt, preferred_element_type=F32)
        for hd in range(PROJ_GROUP // HEAD_DIM):
            r0 = lo + hd * HEAD_DIM
            qh = qT[hd * HEAD_DIM:(hd + 1) * HEAD_DIM, :]
            qh = qh * lax.rsqrt(jnp.mean(qh * qh, axis=0, keepdims=True) + EPS) * qmul_ref[...]
            q1, q2 = qh[:half, :], qh[half:, :]
            qT_ref[r0:r0 + half, :] = (q1 * cosT - q2 * sinT).astype(BF16)
            qT_ref[r0 + half:r0 + HEAD_DIM, :] = (q2 * cosT + q1 * sinT).astype(BF16)

        k = jnp.dot(h, wk_ref[:, lo:lo + PROJ_GROUP], preferred_element_type=F32)
        for hd in range(PROJ_GROUP // HEAD_DIM):
            r0 = lo + hd * HEAD_DIM
            kh = k[:, hd * HEAD_DIM:(hd + 1) * HEAD_DIM]
            kh = kh * _rms_scale(kh) * kmul_ref[...]
            kh = kh * cos2 + pltpu.roll(kh, half, 1) * sin2
            k_ref[:, r0:r0 + HEAD_DIM] = kh.astype(BF16)
            for blk in range(tile // MOBA_BLOCK):
                kmean_ref[blk:blk + 1, r0:r0 + HEAD_DIM] = jnp.mean(
                    kh[blk * MOBA_BLOCK:(blk + 1) * MOBA_BLOCK, :], axis=0, keepdims=True)

        vT_ref[lo:lo + PROJ_GROUP, :] = lax.dot_general(
            wvT_ref[lo:lo + PROJ_GROUP, :], h, nt, preferred_element_type=F32).astype(BF16)
        gate_ref[:, lo:lo + PROJ_GROUP] = _silu(
            jnp.dot(h, wz_ref[:, lo:lo + PROJ_GROUP], preferred_element_type=F32)).astype(BF16)


def _rope_tables(S):
    inv = ROPE_THETA ** (-jnp.arange(0, HEAD_DIM, 2, dtype=F32) / HEAD_DIM)
    ang = jnp.arange(S, dtype=F32)[:, None] * inv[None, :]
    return jnp.cos(ang), jnp.sin(ang)


def _moba_proj(x, norm_g, w_in, q_norm, k_norm):
    B, S, _ = x.shape
    tile = SEQ_TILE
    nblk = tile // MOBA_BLOCK
    wq, wk, wv, wz = jnp.split(w_in.astype(BF16), 4, axis=1)
    cos, sin = _rope_tables(S)
    q_gain = q_norm * (HEAD_DIM ** -0.5 * math.log2(math.e))
    qmul = jnp.broadcast_to(q_gain[:, None], (HEAD_DIM, tile))
    kmul = k_norm.reshape(1, HEAD_DIM)
    tok = lambda b, t: (b, t, 0)
    feat = lambda b, t: (b, 0, t)
    return pl.pallas_call(
        functools.partial(_moba_proj_kernel, tile=tile),
        grid=(B, S // tile),
        in_specs=[
            pl.BlockSpec((None, tile, D_MODEL), tok),
            _const_spec((1, D_MODEL)),
            _const_spec((D_INNER, D_MODEL)),
            _const_spec((D_MODEL, D_INNER)),
            _const_spec((D_INNER, D_MODEL)),
            _const_spec((D_MODEL, D_INNER)),
            _const_spec((HEAD_DIM, tile)),
            _const_spec((1, HEAD_DIM)),
            pl.BlockSpec((HEAD_DIM // 2, tile), lambda b, t: (0, t)),
            pl.BlockSpec((HEAD_DIM // 2, tile), lambda b, t: (0, t)),
            pl.BlockSpec((tile, HEAD_DIM), lambda b, t: (t, 0)),
            pl.BlockSpec((tile, HEAD_DIM), lambda b, t: (t, 0)),
        ],
        out_specs=[
            pl.BlockSpec((None, D_INNER, tile), feat),
            pl.BlockSpec((None, tile, D_INNER), tok),
            pl.BlockSpec((None, D_INNER, tile), feat),
            pl.BlockSpec((None, tile, D_INNER), tok),
            pl.BlockSpec((None, None, nblk, D_INNER), lambda b, t: (b, t, 0, 0)),
        ],
        out_shape=[
            jax.ShapeDtypeStruct((B, D_INNER, S), BF16),
            jax.ShapeDtypeStruct((B, S, D_INNER), BF16),
            jax.ShapeDtypeStruct((B, D_INNER, S), BF16),
            jax.ShapeDtypeStruct((B, S, D_INNER), BF16),
            jax.ShapeDtypeStruct((B, S // tile, nblk, D_INNER), F32),
        ],
        compiler_params=pltpu.CompilerParams(
            dimension_semantics=("arbitrary", "arbitrary"), vmem_limit_bytes=VMEM_LIMIT),
        name="moba_proj",
    )(x, norm_g.reshape(1, D_MODEL), wq.T, wk, wv.T, wz, qmul, kmul,
      cos.T, sin.T, jnp.concatenate([cos, cos], axis=1), jnp.concatenate([-sin, sin], axis=1))


CHAINS_PER_STEP = 64
SCORE_LEAD = 2
OWN_LEAD = 3
TILES_PER_STEP = 16
ONES_ROWS = 16


def _chain_tables(n_blocks):
    chains = []
    for i in range(1, n_blocks):
        js = list(range(i)) + ([i] if i % 2 else [])
        chains += [(i, js[n], js[n + 1]) for n in range(0, len(js), 2)]
    chains.sort(key=lambda c: (c[1], c[0]))
    return tuple(np.asarray([c[n] for c in chains], np.int32) for n in range(3))


def _moba_attn_kernel(qi_ref, ka_ref, kb_ref, zero_ref, qT_ref, k_ref, vT_ref, kmean_ref, gate_ref, y_ref,
                      bias_ref, m_ref, l_ref, acc_ref, *x_refs, n_blocks, n_chains):
    def block(i):
        return pl.ds(pl.multiple_of(i * MOBA_BLOCK, MOBA_BLOCK), MOBA_BLOCK)

    km = kmean_ref[...]
    km_hi = km.astype(BF16)
    km_lo = (km - km_hi.astype(F32)).astype(BF16)
    blk = lax.broadcasted_iota(jnp.int32, (n_blocks, MOBA_BLOCK), 0)
    key_pos = lax.broadcasted_iota(jnp.int32, (MOBA_BLOCK, MOBA_BLOCK), 0)
    qry_pos = lax.broadcasted_iota(jnp.int32, (MOBA_BLOCK, MOBA_BLOCK), 1)
    causal = key_pos <= qry_pos
    ones = jnp.ones((ONES_ROWS, 2 * MOBA_BLOCK), BF16)

    def choose(i, qT):
        gate = (jnp.dot(km_hi, qT, preferred_element_type=F32)
                + jnp.dot(km_lo, qT, preferred_element_type=F32))
        past = blk < i
        cand = jnp.where(past, gate, -jnp.inf)
        chosen = jnp.zeros((n_blocks, MOBA_BLOCK), jnp.bool_)
        for _ in range(MOBA_TOPK):
            top = jnp.max(cand, axis=0, keepdims=True)
            first = jnp.min(jnp.where(cand == top, blk, n_blocks), axis=0, keepdims=True)
            pick = blk == first
            chosen = jnp.logical_or(chosen, jnp.logical_and(pick, past))
            cand = jnp.where(pick, -jnp.inf, cand)
        bias_ref[:, block(i)] = jnp.where(chosen, 0.0, NEG_BIG).astype(F32)

    def own_step(g, carry):
        s = {}
        for u in range(TILES_PER_STEP + OWN_LEAD):
            if u < TILES_PER_STEP:
                i = g * TILES_PER_STEP + u
                qT = qT_ref[:, block(i)]
                s[u] = jnp.dot(k_ref[block(i), :], qT, preferred_element_type=F32)
                choose(i, qT)
            if u >= OWN_LEAD:
                i = g * TILES_PER_STEP + u - OWN_LEAD
                su = jnp.where(causal, s.pop(u - OWN_LEAD), NEG_BIG)
                c = jnp.max(su, axis=0, keepdims=True)
                p = jnp.exp2(su - c).astype(BF16)
                v_aug = jnp.concatenate([vT_ref[:, block(i)], ones[:, :MOBA_BLOCK]], axis=0)
                o_aug = jnp.dot(v_aug, p, preferred_element_type=F32)
                m_ref[i] = c
                l_ref[i] = o_aug[HEAD_DIM:HEAD_DIM + 1, :]
                acc_ref[i] = o_aug[:HEAD_DIM, :]
        return carry

    lax.fori_loop(0, n_blocks // TILES_PER_STEP, own_step, 0)


    def chain_step(g, carry):
        ids, s = {}, {}
        for u in range(CHAINS_PER_STEP + SCORE_LEAD):
            if u < CHAINS_PER_STEP:
                idx = g * CHAINS_PER_STEP + u
                ids[u] = i, ja, jb = qi_ref[idx], ka_ref[idx], kb_ref[idx]
                qT = qT_ref[:, block(i)]
                m0 = m_ref[i]
                xa = jnp.dot(k_ref[block(ja), :], qT, preferred_element_type=F32) - (
                    m0 - bias_ref[pl.ds(ja, 1), block(i)])
                xb = jnp.dot(k_ref[block(jb), :], qT, preferred_element_type=F32) - (
                    m0 - bias_ref[pl.ds(jb, 1), block(i)])
                top = jnp.maximum(jnp.max(xa, axis=0, keepdims=True), jnp.max(xb, axis=0, keepdims=True))
                x_ref = x_refs[u % len(x_refs)]
                x_ref[zero_ref[0], :MOBA_BLOCK, :] = xa.astype(BF16)
                x_ref[zero_ref[0], MOBA_BLOCK:, :] = xb.astype(BF16)
                s[u] = (m0, top)
            if u >= SCORE_LEAD:
                i, ja, jb = ids.pop(u - SCORE_LEAD)
                m0, top = s.pop(u - SCORE_LEAD)
                m_old = m_ref[i]
                shift = jnp.maximum(m_old - m0, top).astype(BF16)
                m_new = m0 + shift.astype(F32)
                p = jnp.exp2(x_refs[(u - SCORE_LEAD) % len(x_refs)][zero_ref[0]] - shift)
                v_aug = jnp.concatenate(
                    [jnp.concatenate([vT_ref[:, block(ja)], vT_ref[:, block(jb)]], axis=1), ones], axis=0)
                o_aug = jnp.dot(v_aug, p, preferred_element_type=F32)
                a_old = jnp.exp2(m_old - m_new)
                m_ref[i] = m_new
                l_ref[i] = a_old * l_ref[i] + o_aug[HEAD_DIM:HEAD_DIM + 1, :]
                acc_ref[i] = a_old * acc_ref[i] + o_aug[:HEAD_DIM, :]
        return carry

    lax.fori_loop(0, n_chains // CHAINS_PER_STEP, chain_step, 0)

    def finish_step(g, carry):
        for u in range(TILES_PER_STEP):
            i = g * TILES_PER_STEP + u
            oT = acc_ref[i] * (1.0 / l_ref[i])
            y_ref[block(i), :] = (oT.T * gate_ref[block(i), :].astype(F32)).astype(BF16)
        return carry

    lax.fori_loop(0, n_blocks // TILES_PER_STEP, finish_step, 0)


def _moba_attn(qT, k, vT, kmean, gate):
    B, S, _ = k.shape
    n_blocks = S // MOBA_BLOCK
    tables = _chain_tables(n_blocks)
    n_chains = len(tables[0])
    assert n_chains % CHAINS_PER_STEP == 0
    assert n_blocks % TILES_PER_STEP == 0
    grid_spec = pltpu.PrefetchScalarGridSpec(
        num_scalar_prefetch=4,
        grid=(B, N_HEADS),
        in_specs=[
            pl.BlockSpec((None, HEAD_DIM, S), lambda b, h, *_: (b, h, 0)),
            pl.BlockSpec((None, S, HEAD_DIM), lambda b, h, *_: (b, 0, h)),
            pl.BlockSpec((None, HEAD_DIM, S), lambda b, h, *_: (b, h, 0)),
            pl.BlockSpec((None, n_blocks, HEAD_DIM), lambda b, h, *_: (b, 0, h)),
            pl.BlockSpec((None, S, HEAD_DIM), lambda b, h, *_: (b, 0, h)),
        ],
        out_specs=pl.BlockSpec((None, S, HEAD_DIM), lambda b, h, *_: (b, 0, h)),
        scratch_shapes=[
            pltpu.VMEM((n_blocks, S), F32),
            pltpu.VMEM((n_blocks, 1, MOBA_BLOCK), F32),
            pltpu.VMEM((n_blocks, 1, MOBA_BLOCK), F32),
            pltpu.VMEM((n_blocks, HEAD_DIM, MOBA_BLOCK), F32),
        ] + [pltpu.VMEM((1, 2 * MOBA_BLOCK, MOBA_BLOCK), BF16)] * (SCORE_LEAD + 1),
    )
    return pl.pallas_call(
        functools.partial(_moba_attn_kernel, n_blocks=n_blocks, n_chains=n_chains),
        grid_spec=grid_spec,
        out_shape=jax.ShapeDtypeStruct((B, S, D_INNER), BF16),
        compiler_params=pltpu.CompilerParams(
            dimension_semantics=("arbitrary", "arbitrary"), vmem_limit_bytes=VMEM_LIMIT),
        name="moba_attn",
    )(*(jnp.asarray(t) for t in tables), jnp.zeros((1,), jnp.int32), qT, k, vT, kmean, gate)


def _out_proj_kernel(x_ref, y_ref, w_ref, o_ref):
    o_ref[...] = x_ref[...] + jnp.dot(y_ref[...], w_ref[...], preferred_element_type=F32)


def _out_proj(x, y, w_out):
    B, S, _ = x.shape
    rows = B * S
    return pl.pallas_call(
        _out_proj_kernel,
        grid=(rows // OUT_TILE,),
        in_specs=[
            pl.BlockSpec((OUT_TILE, D_MODEL), lambda t: (t, 0)),
            pl.BlockSpec((OUT_TILE, D_INNER), lambda t: (t, 0)),
            _const_spec((D_INNER, D_MODEL)),
        ],
        out_specs=pl.BlockSpec((OUT_TILE, D_MODEL), lambda t: (t, 0)),
        out_shape=jax.ShapeDtypeStruct((rows, D_MODEL), F32),
        compiler_params=pltpu.CompilerParams(
            dimension_semantics=("arbitrary",), vmem_limit_bytes=VMEM_LIMIT),
        name="moba_out_proj",
    )(x.reshape(rows, D_MODEL), y.reshape(rows, D_INNER), w_out.astype(BF16)).reshape(B, S, D_MODEL)


def kernel(x, norm0, w_in0, w_pool0, pool_scale0, w_out0, norm1, w_in1, q_norm1, k_norm1, w_out1):
    B, S, _ = x.shape
    x1 = _pool_layer(x, norm0, w_in0, w_pool0, pool_scale0, w_out0)
    qT, k, vT, gate, kmean = _moba_proj(x1, norm1, w_in1, q_norm1, k_norm1)
    y = _moba_attn(qT, k, vT, kmean.reshape(B, S // MOBA_BLOCK, D_INNER), gate)
    return _out_proj(x1, y, w_out1)
```

```python
import functools
import math

import jax
import jax.numpy as jnp
import numpy as np
from jax import lax
from jax.experimental import pallas as pl
from jax.experimental.pallas import tpu as pltpu

D_MODEL = 1024
D_INNER = 2 * D_MODEL
POOL_WINDOWS = (2, 4, 8, 16)
POOL_GROUP_DIM = D_INNER // len(POOL_WINDOWS)
HEAD_DIM = 128
N_HEADS = D_INNER // HEAD_DIM
MOBA_BLOCK = 256
MOBA_TOPK = 3
ROPE_THETA = 10000.0
EPS = 1e-6

POOL_HALO = 16
NEG_BIG = -1e30
VMEM_LIMIT = 56 * 1024 * 1024
SEQ_TILE = 512
POOL_TILE = 1024
OUT_TILE = 1024
PROJ_GROUP = 256

F32 = jnp.float32
BF16 = jnp.bfloat16


def _const_spec(shape):
    nd = len(shape)
    return pl.BlockSpec(shape, lambda *_: (0,) * nd, pipeline_mode=pl.Buffered(1))


def _rms_scale(x):
    return lax.rsqrt(jnp.mean(x * x, axis=-1, keepdims=True) + EPS)


def _silu(z):
    return z / (1.0 + jnp.exp(-z))


def _pool_layer_kernel(x_ref, g_ref, win_ref, wpool_ref, pscale_ref, wout_ref, o_ref, ubuf_ref, *, tile):
    t = pl.program_id(1)
    x = x_ref[...]
    h = (x * _rms_scale(x) * g_ref[...]).astype(BF16)

    @pl.when(t == 0)
    def _():
        ubuf_ref[0:POOL_HALO, :] = jnp.zeros((POOL_HALO, D_INNER), F32)

    pos = t * tile + lax.broadcasted_iota(jnp.int32, (tile, 1), 0) + 1
    y = []
    for g, w in enumerate(POOL_WINDOWS):
        lo, hi = g * POOL_GROUP_DIM, (g + 1) * POOL_GROUP_DIM
        u = jnp.dot(h, win_ref[:, lo:hi], preferred_element_type=F32)
        ubuf_ref[POOL_HALO:POOL_HALO + tile, lo:hi] = u
        ext = ubuf_ref[:, lo:hi]
        acc, width = ext, 1
        while width < w:
            acc = acc + pltpu.roll(acc, width, 0)
            width *= 2
        inv_cnt = 1.0 / jnp.minimum(pos, w).astype(F32)
        pooled = acc[POOL_HALO:, :] * inv_cnt - u
        mixed = jnp.dot(pooled.astype(BF16), wpool_ref[g], preferred_element_type=F32)
        ubuf_ref[0:POOL_HALO, lo:hi] = u[tile - POOL_HALO:, :]
        z = jnp.dot(h, win_ref[:, D_INNER + lo:D_INNER + hi], preferred_element_type=F32)
        y.append((mixed * pscale_ref[:, lo:hi] * _silu(z)).astype(BF16))

    y = jnp.concatenate(y, axis=-1)
    o_ref[...] = x + jnp.dot(y, wout_ref[...], preferred_element_type=F32)


def _pool_layer(x, norm_g, w_in, w_pool, pool_scale, w_out):
    B, S, _ = x.shape
    tile = POOL_TILE
    return pl.pallas_call(
        functools.partial(_pool_layer_kernel, tile=tile),
        grid=(B, S // tile),
        in_specs=[
            pl.BlockSpec((None, tile, D_MODEL), lambda b, t: (b, t, 0)),
            _const_spec((1, D_MODEL)),
            _const_spec((D_MODEL, 2 * D_INNER)),
            _const_spec((len(POOL_WINDOWS), POOL_GROUP_DIM, POOL_GROUP_DIM)),
            _const_spec((1, D_INNER)),
            _const_spec((D_INNER, D_MODEL)),
        ],
        out_specs=pl.BlockSpec((None, tile, D_MODEL), lambda b, t: (b, t, 0)),
        out_shape=jax.ShapeDtypeStruct((B, S, D_MODEL), F32),
        scratch_shapes=[pltpu.VMEM((POOL_HALO + tile, D_INNER), F32)],
        compiler_params=pltpu.CompilerParams(
            dimension_semantics=("arbitrary", "arbitrary"), vmem_limit_bytes=VMEM_LIMIT),
        name="pool_layer",
    )(x, norm_g.reshape(1, D_MODEL), w_in.astype(BF16), w_pool.astype(BF16),
      pool_scale.reshape(1, D_INNER), w_out.astype(BF16))


def _moba_proj_kernel(x_ref, g_ref, wqT_ref, wk_ref, wvT_ref, wz_ref, qmul_ref, kmul_ref,
                      cosT_ref, sinT_ref, cos_ref, sin_ref,
                      qT_ref, k_ref, vT_ref, gate_ref, kmean_ref, *, tile):
    x = x_ref[...]
    h = (x * _rms_scale(x) * g_ref[...]).astype(BF16)
    nt = (((1,), (1,)), ((), ()))

    cosT, sinT = cosT_ref[...], sinT_ref[...]
    cos2, sin2 = cos_ref[...], sin_ref[...]
    half = HEAD_DIM // 2
    for grp in range(D_INNER // PROJ_GROUP):
        lo = grp * PROJ_GROUP
        qT = lax.dot_general(wqT_ref[lo:lo + PROJ_GROUP, :], h, nt, preferred_element_type=F32)
        for hd in range(PROJ_GROUP // HEAD_DIM):
            r0 = lo + hd * HEAD_DIM
            qh = qT[hd * HEAD_DIM:(hd + 1) * HEAD_DIM, :]
            qh = qh * lax.rsqrt(jnp.mean(qh * qh, axis=0, keepdims=True) + EPS) * qmul_ref[...]
            q1, q2 = qh[:half, :], qh[half:, :]
            qT_ref[r0:r0 + half, :] = (q1 * cosT - q2 * sinT).astype(BF16)
            qT_ref[r0 + half:r0 + HEAD_DIM, :] = (q2 * cosT + q1 * sinT).astype(BF16)

        k = jnp.dot(h, wk_ref[:, lo:lo + PROJ_GROUP], preferred_element_type=F32)
        for hd in range(PROJ_GROUP // HEAD_DIM):
            r0 = lo + hd * HEAD_DIM
            kh = k[:, hd * HEAD_DIM:(hd + 1) * HEAD_DIM]
            kh = kh * _rms_scale(kh) * kmul_ref[...]
            kh = kh * cos2 + pltpu.roll(kh, half, 1) * sin2
            k_ref[:, r0:r0 + HEAD_DIM] = kh.astype(BF16)
            for blk in range(tile // MOBA_BLOCK):
                kmean_ref[blk:blk + 1, r0:r0 + HEAD_DIM] = jnp.mean(
                    kh[blk * MOBA_BLOCK:(blk + 1) * MOBA_BLOCK, :], axis=0, keepdims=True)

        vT_ref[lo:lo + PROJ_GROUP, :] = lax.dot_general(
            wvT_ref[lo:lo + PROJ_GROUP, :], h, nt, preferred_element_type=F32).astype(BF16)
        gate_ref[:, lo:lo + PROJ_GROUP] = _silu(
            jnp.dot(h, wz_ref[:, lo:lo + PROJ_GROUP], preferred_element_type=F32)).astype(BF16)


def _rope_tables(S):
    inv = ROPE_THETA ** (-jnp.arange(0, HEAD_DIM, 2, dtype=F32) / HEAD_DIM)
    ang = jnp.arange(S, dtype=F32)[:, None] * inv[None, :]
    return jnp.cos(ang), jnp.sin(ang)


def _moba_proj(x, norm_g, w_in, q_norm, k_norm):
    B, S, _ = x.shape
    tile = SEQ_TILE
    nblk = tile // MOBA_BLOCK
    wq, wk, wv, wz = jnp.split(w_in.astype(BF16), 4, axis=1)
    cos, sin = _rope_tables(S)
    q_gain = q_norm * (HEAD_DIM ** -0.5 * math.log2(math.e))
    qmul = jnp.broadcast_to(q_gain[:, None], (HEAD_DIM, tile))
    kmul = k_norm.reshape(1, HEAD_DIM)
    tok = lambda b, t: (b, t, 0)
    feat = lambda b, t: (b, 0, t)
    return pl.pallas_call(
        functools.partial(_moba_proj_kernel, tile=tile),
        grid=(B, S // tile),
        in_specs=[
            pl.BlockSpec((None, tile, D_MODEL), tok),
            _const_spec((1, D_MODEL)),
            _const_spec((D_INNER, D_MODEL)),
            _const_spec((D_MODEL, D_INNER)),
            _const_spec((D_INNER, D_MODEL)),
            _const_spec((D_MODEL, D_INNER)),
            _const_spec((HEAD_DIM, tile)),
            _const_spec((1, HEAD_DIM)),
            pl.BlockSpec((HEAD_DIM // 2, tile), lambda b, t: (0, t)),
            pl.BlockSpec((HEAD_DIM // 2, tile), lambda b, t: (0, t)),
            pl.BlockSpec((tile, HEAD_DIM), lambda b, t: (t, 0)),
            pl.BlockSpec((tile, HEAD_DIM), lambda b, t: (t, 0)),
        ],
        out_specs=[
            pl.BlockSpec((None, D_INNER, tile), feat),
            pl.BlockSpec((None, tile, D_INNER), tok),
            pl.BlockSpec((None, D_INNER, tile), feat),
            pl.BlockSpec((None, tile, D_INNER), tok),
            pl.BlockSpec((None, None, nblk, D_INNER), lambda b, t: (b, t, 0, 0)),
        ],
        out_shape=[
            jax.ShapeDtypeStruct((B, D_INNER, S), BF16),
            jax.ShapeDtypeStruct((B, S, D_INNER), BF16),
            jax.ShapeDtypeStruct((B, D_INNER, S), BF16),
            jax.ShapeDtypeStruct((B, S, D_INNER), BF16),
            jax.ShapeDtypeStruct((B, S // tile, nblk, D_INNER), F32),
        ],
        compiler_params=pltpu.CompilerParams(
            dimension_semantics=("arbitrary", "arbitrary"), vmem_limit_bytes=VMEM_LIMIT),
        name="moba_proj",
    )(x, norm_g.reshape(1, D_MODEL), wq.T, wk, wv.T, wz, qmul, kmul,
      cos.T, sin.T, jnp.concatenate([cos, cos], axis=1), jnp.concatenate([-sin, sin], axis=1))


CHAINS_PER_STEP = 64
SCORE_LEAD = 2
OWN_LEAD = 3
TILES_PER_STEP = 32
ONES_ROWS = 16


def _chain_tables(n_blocks):
    chains = []
    for i in range(1, n_blocks):
        js = list(range(i)) + ([i] if i % 2 else [])
        chains += [(i, js[n], js[n + 1]) for n in range(0, len(js), 2)]
    chains.sort(key=lambda c: (c[1], c[0]))
    return tuple(np.asarray([c[n] for c in chains], np.int32) for n in range(3))


def _moba_attn_kernel(qi_ref, ka_ref, kb_ref, zero_ref, qT_ref, k_ref, vT_ref, kmean_ref, gate_ref, y_ref,
                      bias_ref, m_ref, l_ref, acc_ref, *x_refs, n_blocks, n_chains):
    def block(i):
        return pl.ds(pl.multiple_of(i * MOBA_BLOCK, MOBA_BLOCK), MOBA_BLOCK)

    km = kmean_ref[...]
    km_hi = km.astype(BF16)
    km_lo = (km - km_hi.astype(F32)).astype(BF16)
    blk = lax.broadcasted_iota(jnp.int32, (n_blocks, MOBA_BLOCK), 0)
    key_pos = lax.broadcasted_iota(jnp.int32, (MOBA_BLOCK, MOBA_BLOCK), 0)
    qry_pos = lax.broadcasted_iota(jnp.int32, (MOBA_BLOCK, MOBA_BLOCK), 1)
    causal = key_pos <= qry_pos
    ones = jnp.ones((ONES_ROWS, 2 * MOBA_BLOCK), BF16)

    def choose(i, qT):
        gate = (jnp.dot(km_hi, qT, preferred_element_type=F32)
                + jnp.dot(km_lo, qT, preferred_element_type=F32))
        past = blk < i
        cand = jnp.where(past, gate, -jnp.inf)
        chosen = jnp.zeros((n_blocks, MOBA_BLOCK), jnp.bool_)
        for _ in range(MOBA_TOPK):
            top = jnp.max(cand, axis=0, keepdims=True)
            first = jnp.min(jnp.where(cand == top, blk, n_blocks), axis=0, keepdims=True)
            pick = blk == first
            chosen = jnp.logical_or(chosen, jnp.logical_and(pick, past))
            cand = jnp.where(pick, -jnp.inf, cand)
        bias_ref[:, block(i)] = jnp.where(chosen, 0.0, NEG_BIG).astype(F32)

    def own_step(g, carry):
        s = {}
        for u in range(TILES_PER_STEP + OWN_LEAD):
            if u < TILES_PER_STEP:
                i = g * TILES_PER_STEP + u
                qT = qT_ref[:, block(i)]
                s[u] = jnp.dot(k_ref[block(i), :], qT, preferred_element_type=F32)
                choose(i, qT)
            if u >= OWN_LEAD:
                i = g * TILES_PER_STEP + u - OWN_LEAD
                su = jnp.where(causal, s.pop(u - OWN_LEAD), NEG_BIG)
                c = jnp.max(su, axis=0, keepdims=True)
                p = jnp.exp2(su - c).astype(BF16)
                v_aug = jnp.concatenate([vT_ref[:, block(i)], ones[:, :MOBA_BLOCK]], axis=0)
                o_aug = jnp.dot(v_aug, p, preferred_element_type=F32)
                m_ref[i] = c
                l_ref[i] = o_aug[HEAD_DIM:HEAD_DIM + 1, :]
                acc_ref[i] = o_aug[:HEAD_DIM, :]
        return carry

    lax.fori_loop(0, n_blocks // TILES_PER_STEP, own_step, 0)


    def chain_step(g, carry):
        ids, s = {}, {}
        for u in range(CHAINS_PER_STEP + SCORE_LEAD):
            if u < CHAINS_PER_STEP:
                idx = g * CHAINS_PER_STEP + u
                ids[u] = i, ja, jb = qi_ref[idx], ka_ref[idx], kb_ref[idx]
                qT = qT_ref[:, block(i)]
                m0 = m_ref[i]
                xa = jnp.dot(k_ref[block(ja), :], qT, preferred_element_type=F32) - (
                    m0 - bias_ref[pl.ds(ja, 1), block(i)])
                xb = jnp.dot(k_ref[block(jb), :], qT, preferred_element_type=F32) - (
                    m0 - bias_ref[pl.ds(jb, 1), block(i)])
                top = jnp.maximum(jnp.max(xa, axis=0, keepdims=True), jnp.max(xb, axis=0, keepdims=True))
                x_ref = x_refs[u % len(x_refs)]
                x_ref[zero_ref[0], :MOBA_BLOCK, :] = xa.astype(BF16)
                x_ref[zero_ref[0], MOBA_BLOCK:, :] = xb.astype(BF16)
                s[u] = (m0, top)
            if u >= SCORE_LEAD:
                i, ja, jb = ids.pop(u - SCORE_LEAD)
                m0, top = s.pop(u - SCORE_LEAD)
                m_old = m_ref[i]
                shift = jnp.maximum(m_old - m0, top).astype(BF16)
                m_new = m0 + shift.astype(F32)
                p = jnp.exp2(x_refs[(u - SCORE_LEAD) % len(x_refs)][zero_ref[0]] - shift)
                v_aug = jnp.concatenate(
                    [jnp.concatenate([vT_ref[:, block(ja)], vT_ref[:, block(jb)]], axis=1), ones], axis=0)
                o_aug = jnp.dot(v_aug, p, preferred_element_type=F32)
                a_old = jnp.exp2(m_old - m_new)
                m_ref[i] = m_new
                l_ref[i] = a_old * l_ref[i] + o_aug[HEAD_DIM:HEAD_DIM + 1, :]
                acc_ref[i] = a_old * acc_ref[i] + o_aug[:HEAD_DIM, :]
        return carry

    lax.fori_loop(0, n_chains // CHAINS_PER_STEP, chain_step, 0)

    def finish_step(g, carry):
        for u in range(TILES_PER_STEP):
            i = g * TILES_PER_STEP + u
            oT = acc_ref[i] * (1.0 / l_ref[i])
            y_ref[block(i), :] = (oT.T * gate_ref[block(i), :].astype(F32)).astype(BF16)
        return carry

    lax.fori_loop(0, n_blocks // TILES_PER_STEP, finish_step, 0)


def _moba_attn(qT, k, vT, kmean, gate):
    B, S, _ = k.shape
    n_blocks = S // MOBA_BLOCK
    tables = _chain_tables(n_blocks)
    n_chains = len(tables[0])
    assert n_chains % CHAINS_PER_STEP == 0
    assert n_blocks % TILES_PER_STEP == 0
    grid_spec = pltpu.PrefetchScalarGridSpec(
        num_scalar_prefetch=4,
        grid=(B, N_HEADS),
        in_specs=[
            pl.BlockSpec((None, HEAD_DIM, S), lambda b, h, *_: (b, h, 0)),
            pl.BlockSpec((None, S, HEAD_DIM), lambda b, h, *_: (b, 0, h)),
            pl.BlockSpec((None, HEAD_DIM, S), lambda b, h, *_: (b, h, 0)),
            pl.BlockSpec((None, n_blocks, HEAD_DIM), lambda b, h, *_: (b, 0, h)),
            pl.BlockSpec((None, S, HEAD_DIM), lambda b, h, *_: (b, 0, h)),
        ],
        out_specs=pl.BlockSpec((None, S, HEAD_DIM), lambda b, h, *_: (b, 0, h)),
        scratch_shapes=[
            pltpu.VMEM((n_blocks, S), F32),
            pltpu.VMEM((n_blocks, 1, MOBA_BLOCK), F32),
            pltpu.VMEM((n_blocks, 1, MOBA_BLOCK), F32),
            pltpu.VMEM((n_blocks, HEAD_DIM, MOBA_BLOCK), F32),
        ] + [pltpu.VMEM((1, 2 * MOBA_BLOCK, MOBA_BLOCK), BF16)] * (SCORE_LEAD + 1),
    )
    return pl.pallas_call(
        functools.partial(_moba_attn_kernel, n_blocks=n_blocks, n_chains=n_chains),
        grid_spec=grid_spec,
        out_shape=jax.ShapeDtypeStruct((B, S, D_INNER), BF16),
        compiler_params=pltpu.CompilerParams(
            dimension_semantics=("arbitrary", "arbitrary"), vmem_limit_bytes=VMEM_LIMIT),
        name="moba_attn",
    )(*(jnp.asarray(t) for t in tables), jnp.zeros((1,), jnp.int32), qT, k, vT, kmean, gate)


def _out_proj_kernel(x_ref, y_ref, w_ref, o_ref):
    o_ref[...] = x_ref[...] + jnp.dot(y_ref[...], w_ref[...], preferred_element_type=F32)


def _out_proj(x, y, w_out):
    B, S, _ = x.shape
    rows = B * S
    return pl.pallas_call(
        _out_proj_kernel,
        grid=(rows // OUT_TILE,),
        in_specs=[
            pl.BlockSpec((OUT_TILE, D_MODEL), lambda t: (t, 0)),
            pl.BlockSpec((OUT_TILE, D_INNER), lambda t: (t, 0)),
            _const_spec((D_INNER, D_MODEL)),
        ],
        out_specs=pl.BlockSpec((OUT_TILE, D_MODEL), lambda t: (t, 0)),
        out_shape=jax.ShapeDtypeStruct((rows, D_MODEL), F32),
        compiler_params=pltpu.CompilerParams(
            dimension_semantics=("arbitrary",), vmem_limit_bytes=VMEM_LIMIT),
        name="moba_out_proj",
    )(x.reshape(rows, D_MODEL), y.reshape(rows, D_INNER), w_out.astype(BF16)).reshape(B, S, D_MODEL)


def kernel(x, norm0, w_in0, w_pool0, pool_scale0, w_out0, norm1, w_in1, q_norm1, k_norm1, w_out1):
    B, S, _ = x.shape
    x1 = _pool_layer(x, norm0, w_in0, w_pool0, pool_scale0, w_out0)
    qT, k, vT, gate, kmean = _moba_proj(x1, norm1, w_in1, q_norm1, k_norm1)
    y = _moba_attn(qT, k, vT, kmean.reshape(B, S // MOBA_BLOCK, D_INNER), gate)
    return _out_proj(x1, y, w_out1)
```

```python
import functools
import math

import jax
import jax.numpy as jnp
import numpy as np
from jax import lax
from jax.experimental import pallas as pl
from jax.experimental.pallas import tpu as pltpu

D_MODEL = 1024
D_INNER = 2 * D_MODEL
POOL_WINDOWS = (2, 4, 8, 16)
POOL_GROUP_DIM = D_INNER // len(POOL_WINDOWS)
HEAD_DIM = 128
N_HEADS = D_INNER // HEAD_DIM
MOBA_BLOCK = 256
MOBA_TOPK = 3
ROPE_THETA = 10000.0
EPS = 1e-6

POOL_HALO = 16
NEG_BIG = -1e30
VMEM_LIMIT = 56 * 1024 * 1024
SEQ_TILE = 512
POOL_TILE = 1024
OUT_TILE = 1024
PROJ_GROUP = 256

F32 = jnp.float32
BF16 = jnp.bfloat16


def _const_spec(shape):
    nd = len(shape)
    return pl.BlockSpec(shape, lambda *_: (0,) * nd, pipeline_mode=pl.Buffered(1))


def _rms_scale(x):
    return lax.rsqrt(jnp.mean(x * x, axis=-1, keepdims=True) + EPS)


def _silu(z):
    return z / (1.0 + jnp.exp(-z))


def _pool_layer_kernel(x_ref, g_ref, win_ref, wpool_ref, pscale_ref, wout_ref, o_ref, ubuf_ref, *, tile):
    t = pl.program_id(1)
    x = x_ref[...]
    h = (x * _rms_scale(x) * g_ref[...]).astype(BF16)

    @pl.when(t == 0)
    def _():
        ubuf_ref[0:POOL_HALO, :] = jnp.zeros((POOL_HALO, D_INNER), F32)

    pos = t * tile + lax.broadcasted_iota(jnp.int32, (tile, 1), 0) + 1
    y = []
    for g, w in enumerate(POOL_WINDOWS):
        lo, hi = g * POOL_GROUP_DIM, (g + 1) * POOL_GROUP_DIM
        u = jnp.dot(h, win_ref[:, lo:hi], preferred_element_type=F32)
        ubuf_ref[POOL_HALO:POOL_HALO + tile, lo:hi] = u
        ext = ubuf_ref[:, lo:hi]
        acc, width = ext, 1
        while width < w:
            acc = acc + pltpu.roll(acc, width, 0)
            width *= 2
        inv_cnt = 1.0 / jnp.minimum(pos, w).astype(F32)
        pooled = acc[POOL_HALO:, :] * inv_cnt - u
        mixed = jnp.dot(pooled.astype(BF16), wpool_ref[g], preferred_element_type=F32)
        ubuf_ref[0:POOL_HALO, lo:hi] = u[tile - POOL_HALO:, :]
        z = jnp.dot(h, win_ref[:, D_INNER + lo:D_INNER + hi], preferred_element_type=F32)
        y.append((mixed * pscale_ref[:, lo:hi] * _silu(z)).astype(BF16))

    y = jnp.concatenate(y, axis=-1)
    o_ref[...] = x + jnp.dot(y, wout_ref[...], preferred_element_type=F32)


def _pool_layer(x, norm_g, w_in, w_pool, pool_scale, w_out):
    B, S, _ = x.shape
    tile = POOL_TILE
    return pl.pallas_call(
        functools.partial(_pool_layer_kernel, tile=tile),
        grid=(B, S // tile),
        in_specs=[
            pl.BlockSpec((None, tile, D_MODEL), lambda b, t: (b, t, 0)),
            _const_spec((1, D_MODEL)),
            _const_spec((D_MODEL, 2 * D_INNER)),
            _const_spec((len(POOL_WINDOWS), POOL_GROUP_DIM, POOL_GROUP_DIM)),
            _const_spec((1, D_INNER)),
            _const_spec((D_INNER, D_MODEL)),
        ],
        out_specs=pl.BlockSpec((None, tile, D_MODEL), lambda b, t: (b, t, 0)),
        out_shape=jax.ShapeDtypeStruct((B, S, D_MODEL), F32),
        scratch_shapes=[pltpu.VMEM((POOL_HALO + tile, D_INNER), F32)],
        compiler_params=pltpu.CompilerParams(
            dimension_semantics=("arbitrary", "arbitrary"), vmem_limit_bytes=VMEM_LIMIT),
        name="pool_layer",
    )(x, norm_g.reshape(1, D_MODEL), w_in.astype(BF16), w_pool.astype(BF16),
      pool_scale.reshape(1, D_INNER), w_out.astype(BF16))


def _moba_proj_kernel(x_ref, g_ref, wqT_ref, wk_ref, wvT_ref, wz_ref, qmul_ref, kmul_ref,
                      cosT_ref, sinT_ref, cos_ref, sin_ref,
                      qT_ref, k_ref, vT_ref, gate_ref, kmean_ref, *, tile):
    x = x_ref[...]
    h = (x * _rms_scale(x) * g_ref[...]).astype(BF16)
    nt = (((1,), (1,)), ((), ()))

    cosT, sinT = cosT_ref[...], sinT_ref[...]
    cos2, sin2 = cos_ref[...], sin_ref[...]
    half = HEAD_DIM // 2
    for grp in range(D_INNER // PROJ_GROUP):
        lo = grp * PROJ_GROUP
        qT = lax.dot_general(wqT_ref[lo:lo + PROJ_GROUP, :], h, nt, preferred_element_type=F32)
        for hd in range(PROJ_GROUP // HEAD_DIM):
            r0 = lo + hd * HEAD_DIM
            qh = qT[hd * HEAD_DIM:(hd + 1) * HEAD_DIM, :]
            qh = qh * lax.rsqrt(jnp.mean(qh * qh, axis=0, keepdims=True) + EPS) * qmul_ref[...]
            q1, q2 = qh[:half, :], qh[half:, :]
            qT_ref[r0:r0 + half, :] = (q1 * cosT - q2 * sinT).astype(BF16)
            qT_ref[r0 + half:r0 + HEAD_DIM, :] = (q2 * cosT + q1 * sinT).astype(BF16)

        k = jnp.dot(h, wk_ref[:, lo:lo + PROJ_GROUP], preferred_element_type=F32)
        for hd in range(PROJ_GROUP // HEAD_DIM):
            r0 = lo + hd * HEAD_DIM
            kh = k[:, hd * HEAD_DIM:(hd + 1) * HEAD_DIM]
            kh = kh * _rms_scale(kh) * kmul_ref[...]
            kh = kh * cos2 + pltpu.roll(kh, half, 1) * sin2
            k_ref[:, r0:r0 + HEAD_DIM] = kh.astype(BF16)
            for blk in range(tile // MOBA_BLOCK):
                kmean_ref[blk:blk + 1, r0:r0 + HEAD_DIM] = jnp.mean(
                    kh[blk * MOBA_BLOCK:(blk + 1) * MOBA_BLOCK, :], axis=0, keepdims=True)

        vT_ref[lo:lo + PROJ_GROUP, :] = lax.dot_general(
            wvT_ref[lo:lo + PROJ_GROUP, :], h, nt, preferred_element_type=F32).astype(BF16)
        gate_ref[:, lo:lo + PROJ_GROUP] = _silu(
            jnp.dot(h, wz_ref[:, lo:lo + PROJ_GROUP], preferred_element_type=F32)).astype(BF16)


def _rope_tables(S):
    inv = ROPE_THETA ** (-jnp.arange(0, HEAD_DIM, 2, dtype=F32) / HEAD_DIM)
    ang = jnp.arange(S, dtype=F32)[:, None] * inv[None, :]
    return jnp.cos(ang), jnp.sin(ang)


def _moba_proj(x, norm_g, w_in, q_norm, k_norm):
    B, S, _ = x.shape
    tile = SEQ_TILE
    nblk = tile // MOBA_BLOCK
    wq, wk, wv, wz = jnp.split(w_in.astype(BF16), 4, axis=1)
    cos, sin = _rope_tables(S)
    q_gain = q_norm * (HEAD_DIM ** -0.5 * math.log2(math.e))
    qmul = jnp.broadcast_to(q_gain[:, None], (HEAD_DIM, tile))
    kmul = k_norm.reshape(1, HEAD_DIM)
    tok = lambda b, t: (b, t, 0)
    feat = lambda b, t: (b, 0, t)
    return pl.pallas_call(
        functools.partial(_moba_proj_kernel, tile=tile),
        grid=(B, S // tile),
        in_specs=[
            pl.BlockSpec((None, tile, D_MODEL), tok),
            _const_spec((1, D_MODEL)),
            _const_spec((D_INNER, D_MODEL)),
            _const_spec((D_MODEL, D_INNER)),
            _const_spec((D_INNER, D_MODEL)),
            _const_spec((D_MODEL, D_INNER)),
            _const_spec((HEAD_DIM, tile)),
            _const_spec((1, HEAD_DIM)),
            pl.BlockSpec((HEAD_DIM // 2, tile), lambda b, t: (0, t)),
            pl.BlockSpec((HEAD_DIM // 2, tile), lambda b, t: (0, t)),
            pl.BlockSpec((tile, HEAD_DIM), lambda b, t: (t, 0)),
            pl.BlockSpec((tile, HEAD_DIM), lambda b, t: (t, 0)),
        ],
        out_specs=[
            pl.BlockSpec((None, D_INNER, tile), feat),
            pl.BlockSpec((None, tile, D_INNER), tok),
            pl.BlockSpec((None, D_INNER, tile), feat),
            pl.BlockSpec((None, tile, D_INNER), tok),
            pl.BlockSpec((None, None, nblk, D_INNER), lambda b, t: (b, t, 0, 0)),
        ],
        out_shape=[
            jax.ShapeDtypeStruct((B, D_INNER, S), BF16),
            jax.ShapeDtypeStruct((B, S, D_INNER), BF16),
            jax.ShapeDtypeStruct((B, D_INNER, S), BF16),
            jax.ShapeDtypeStruct((B, S, D_INNER), BF16),
            jax.ShapeDtypeStruct((B, S // tile, nblk, D_INNER), F32),
        ],
        compiler_params=pltpu.CompilerParams(
            dimension_semantics=("arbitrary", "arbitrary"), vmem_limit_bytes=VMEM_LIMIT),
        name="moba_proj",
    )(x, norm_g.reshape(1, D_MODEL), wq.T, wk, wv.T, wz, qmul, kmul,
      cos.T, sin.T, jnp.concatenate([cos, cos], axis=1), jnp.concatenate([-sin, sin], axis=1))


CHAINS_PER_STEP = 128
SCORE_LEAD = 2
OWN_LEAD = 3
TILES_PER_STEP = 32
ONES_ROWS = 16


def _chain_tables(n_blocks):
    chains = []
    for i in range(1, n_blocks):
        js = list(range(i)) + ([i] if i % 2 else [])
        chains += [(i, js[n], js[n + 1]) for n in range(0, len(js), 2)]
    chains.sort(key=lambda c: (c[1], c[0]))
    return tuple(np.asarray([c[n] for c in chains], np.int32) for n in range(3))


def _moba_attn_kernel(qi_ref, ka_ref, kb_ref, zero_ref, qT_ref, k_ref, vT_ref, kmean_ref, gate_ref, y_ref,
                      bias_ref, m_ref, l_ref, acc_ref, *x_refs, n_blocks, n_chains):
    def block(i):
        return pl.ds(pl.multiple_of(i * MOBA_BLOCK, MOBA_BLOCK), MOBA_BLOCK)

    km = kmean_ref[...]
    km_hi = km.astype(BF16)
    km_lo = (km - km_hi.astype(F32)).astype(BF16)
    blk = lax.broadcasted_iota(jnp.int32, (n_blocks, MOBA_BLOCK), 0)
    key_pos = lax.broadcasted_iota(jnp.int32, (MOBA_BLOCK, MOBA_BLOCK), 0)
    qry_pos = lax.broadcasted_iota(jnp.int32, (MOBA_BLOCK, MOBA_BLOCK), 1)
    causal = key_pos <= qry_pos
    ones = jnp.ones((ONES_ROWS, 2 * MOBA_BLOCK), BF16)

    def choose(i, qT):
        gate = (jnp.dot(km_hi, qT, preferred_element_type=F32)
                + jnp.dot(km_lo, qT, preferred_element_type=F32))
        past = blk < i
        cand = jnp.where(past, gate, -jnp.inf)
        chosen = jnp.zeros((n_blocks, MOBA_BLOCK), jnp.bool_)
        for _ in range(MOBA_TOPK):
            top = jnp.max(cand, axis=0, keepdims=True)
            first = jnp.min(jnp.where(cand == top, blk, n_blocks), axis=0, keepdims=True)
            pick = blk == first
            chosen = jnp.logical_or(chosen, jnp.logical_and(pick, past))
            cand = jnp.where(pick, -jnp.inf, cand)
        bias_ref[:, block(i)] = jnp.where(chosen, 0.0, NEG_BIG).astype(F32)

    def own_step(g, carry):
        s = {}
        for u in range(TILES_PER_STEP + OWN_LEAD):
            if u < TILES_PER_STEP:
                i = g * TILES_PER_STEP + u
                qT = qT_ref[:, block(i)]
                s[u] = jnp.dot(k_ref[block(i), :], qT, preferred_element_type=F32)
                choose(i, qT)
            if u >= OWN_LEAD:
                i = g * TILES_PER_STEP + u - OWN_LEAD
                su = jnp.where(causal, s.pop(u - OWN_LEAD), NEG_BIG)
                c = jnp.max(su, axis=0, keepdims=True)
                p = jnp.exp2(su - c).astype(BF16)
                v_aug = jnp.concatenate([vT_ref[:, block(i)], ones[:, :MOBA_BLOCK]], axis=0)
                o_aug = jnp.dot(v_aug, p, preferred_element_type=F32)
                m_ref[i] = c
                l_ref[i] = o_aug[HEAD_DIM:HEAD_DIM + 1, :]
                acc_ref[i] = o_aug[:HEAD_DIM, :]
        return carry

    lax.fori_loop(0, n_blocks // TILES_PER_STEP, own_step, 0)


    def chain_step(g, carry):
        ids, s = {}, {}
        for u in range(CHAINS_PER_STEP + SCORE_LEAD):
            if u < CHAINS_PER_STEP:
                idx = g * CHAINS_PER_STEP + u
                ids[u] = i, ja, jb = qi_ref[idx], ka_ref[idx], kb_ref[idx]
                qT = qT_ref[:, block(i)]
                m0 = m_ref[i]
                xa = jnp.dot(k_ref[block(ja), :], qT, preferred_element_type=F32) - (
                    m0 - bias_ref[pl.ds(ja, 1), block(i)])
                xb = jnp.dot(k_ref[block(jb), :], qT, preferred_element_type=F32) - (
                    m0 - bias_ref[pl.ds(jb, 1), block(i)])
                top = jnp.maximum(jnp.max(xa, axis=0, keepdims=True), jnp.max(xb, axis=0, keepdims=True))
                x_ref = x_refs[u % len(x_refs)]
                x_ref[zero_ref[0], :MOBA_BLOCK, :] = xa.astype(BF16)
                x_ref[zero_ref[0], MOBA_BLOCK:, :] = xb.astype(BF16)
                s[u] = (m0, top)
            if u >= SCORE_LEAD:
                i, ja, jb = ids.pop(u - SCORE_LEAD)
                m0, top = s.pop(u - SCORE_LEAD)
                m_old = m_ref[i]
                shift = jnp.maximum(m_old - m0, top).astype(BF16)
                m_new = m0 + shift.astype(F32)
                p = jnp.exp2(x_refs[(u - SCORE_LEAD) % len(x_refs)][zero_ref[0]] - shift)
                v_aug = jnp.concatenate(
                    [jnp.concatenate([vT_ref[:, block(ja)], vT_ref[:, block(jb)]], axis=1), ones], axis=0)
                o_aug = jnp.dot(v_aug, p, preferred_element_type=F32)
                a_old = jnp.exp2(m_old - m_new)
                m_ref[i] = m_new
                l_ref[i] = a_old * l_ref[i] + o_aug[HEAD_DIM:HEAD_DIM + 1, :]
                acc_ref[i] = a_old * acc_ref[i] + o_aug[:HEAD_DIM, :]
        return carry

    lax.fori_loop(0, n_chains // CHAINS_PER_STEP, chain_step, 0)

    def finish_step(g, carry):
        for u in range(TILES_PER_STEP):
            i = g * TILES_PER_STEP + u
            oT = acc_ref[i] * (1.0 / l_ref[i])
            y_ref[block(i), :] = (oT.T * gate_ref[block(i), :].astype(F32)).astype(BF16)
        return carry

    lax.fori_loop(0, n_blocks // TILES_PER_STEP, finish_step, 0)


def _moba_attn(qT, k, vT, kmean, gate):
    B, S, _ = k.shape
    n_blocks = S // MOBA_BLOCK
    tables = _chain_tables(n_blocks)
    n_chains = len(tables[0])
    assert n_chains % CHAINS_PER_STEP == 0
    assert n_blocks % TILES_PER_STEP == 0
    grid_spec = pltpu.PrefetchScalarGridSpec(
        num_scalar_prefetch=4,
        grid=(B, N_HEADS),
        in_specs=[
            pl.BlockSpec((None, HEAD_DIM, S), lambda b, h, *_: (b, h, 0)),
            pl.BlockSpec((None, S, HEAD_DIM), lambda b, h, *_: (b, 0, h)),
            pl.BlockSpec((None, HEAD_DIM, S), lambda b, h, *_: (b, h, 0)),
            pl.BlockSpec((None, n_blocks, HEAD_DIM), lambda b, h, *_: (b, 0, h)),
            pl.BlockSpec((None, S, HEAD_DIM), lambda b, h, *_: (b, 0, h)),
        ],
        out_specs=pl.BlockSpec((None, S, HEAD_DIM), lambda b, h, *_: (b, 0, h)),
        scratch_shapes=[
            pltpu.VMEM((n_blocks, S), F32),
            pltpu.VMEM((n_blocks, 1, MOBA_BLOCK), F32),
            pltpu.VMEM((n_blocks, 1, MOBA_BLOCK), F32),
            pltpu.VMEM((n_blocks, HEAD_DIM, MOBA_BLOCK), F32),
        ] + [pltpu.VMEM((1, 2 * MOBA_BLOCK, MOBA_BLOCK), BF16)] * (SCORE_LEAD + 1),
    )
    return pl.pallas_call(
        functools.partial(_moba_attn_kernel, n_blocks=n_blocks, n_chains=n_chains),
        grid_spec=grid_spec,
        out_shape=jax.ShapeDtypeStruct((B, S, D_INNER), BF16),
        compiler_params=pltpu.CompilerParams(
            dimension_semantics=("arbitrary", "arbitrary"), vmem_limit_bytes=VMEM_LIMIT),
        name="moba_attn",
    )(*(jnp.asarray(t) for t in tables), jnp.zeros((1,), jnp.int32), qT, k, vT, kmean, gate)


def _out_proj_kernel(x_ref, y_ref, w_ref, o_ref):
    o_ref[...] = x_ref[...] + jnp.dot(y_ref[...], w_ref[...], preferred_element_type=F32)


def _out_proj(x, y, w_out):
    B, S, _ = x.shape
    rows = B * S
    return pl.pallas_call(
        _out_proj_kernel,
        grid=(rows // OUT_TILE,),
        in_specs=[
            pl.BlockSpec((OUT_TILE, D_MODEL), lambda t: (t, 0)),
            pl.BlockSpec((OUT_TILE, D_INNER), lambda t: (t, 0)),
            _const_spec((D_INNER, D_MODEL)),
        ],
        out_specs=pl.BlockSpec((OUT_TILE, D_MODEL), lambda t: (t, 0)),
        out_shape=jax.ShapeDtypeStruct((rows, D_MODEL), F32),
        compiler_params=pltpu.CompilerParams(
            dimension_semantics=("arbitrary",), vmem_limit_bytes=VMEM_LIMIT),
        name="moba_out_proj",
    )(x.reshape(rows, D_MODEL), y.reshape(rows, D_INNER), w_out.astype(BF16)).reshape(B, S, D_MODEL)


def kernel(x, norm0, w_in0, w_pool0, pool_scale0, w_out0, norm1, w_in1, q_norm1, k_norm1, w_out1):
    B, S, _ = x.shape
    x1 = _pool_layer(x, norm0, w_in0, w_pool0, pool_scale0, w_out0)
    qT, k, vT, gate, kmean = _moba_proj(x1, norm1, w_in1, q_norm1, k_norm1)
    y = _moba_attn(qT, k, vT, kmean.reshape(B, S // MOBA_BLOCK, D_INNER), gate)
    return _out_proj(x1, y, w_out1)
```
